```python
import math
import jax, jax.numpy as jnp
from jax import lax
import numpy as np

D_MODEL = 1024
BATCH = 8
SEQ = 4096
DEPTH = 2

PLE_DIM = 256
A_HEADS = 8
A_HEAD_DIM = 64
IDX_HEADS = 4
IDX_DIM = 64
TOPK_MAX = 256
B_HEADS = 8
B_NOPE = 64
B_ROPE = 32
B_V = 64
Q_LORA = 384
KV_LORA = 256
C_CH = 512
CONV_W = 31
N_BRANCH = 3
BRANCH_W = 512
FF = (8 * D_MODEL + 3 * 256 - 1) // (3 * 256) * 256

ROPE_THETA = 10000.0
Q_BLOCK = 128
LN_EPS = 1e-5
RMS_EPS = 1e-6
ALPHA = (2 * DEPTH) ** 0.25
OUT_SCALE = (8 * DEPTH) ** -0.25

IN_SIZES = (A_HEADS * A_HEAD_DIM, A_HEAD_DIM, A_HEAD_DIM,
            IDX_HEADS * IDX_DIM, IDX_DIM, IDX_HEADS,
            Q_LORA, KV_LORA, B_ROPE,
            2 * C_CH,
            N_BRANCH * D_MODEL)
IN_SPLITS = tuple(int(v) for v in np.cumsum(IN_SIZES)[:-1])
N_IN = int(sum(IN_SIZES))

kernel_name = "hybrid_dsa_mla_conformer_deepnorm"


def layer_norm(x, g, b):
    xf = x.astype(jnp.float32)
    mu = xf.mean(-1, keepdims=True)
    var = jnp.square(xf - mu).mean(-1, keepdims=True)
    return ((xf - mu) * lax.rsqrt(var + LN_EPS) * g + b).astype(x.dtype)


def rms_norm(x, g):
    xf = x.astype(jnp.float32)
    return (xf * lax.rsqrt(jnp.mean(xf * xf, -1, keepdims=True) + RMS_EPS) * g).astype(x.dtype)


def rope_tables(positions, dim):
    inv = 1.0 / (ROPE_THETA ** (jnp.arange(0, dim, 2, dtype=jnp.float32) / dim))
    ang = positions.astype(jnp.float32)[..., None] * inv
    return jnp.cos(ang), jnp.sin(ang)


def apply_rope(x, cos, sin):
    shp = cos.shape[:2] + (1,) * (x.ndim - 3) + cos.shape[-1:]
    c, s = cos.reshape(shp), sin.reshape(shp)
    x1, x2 = jnp.split(x.astype(jnp.float32), 2, axis=-1)
    return jnp.concatenate([x1 * c - x2 * s, x2 * c + x1 * s], axis=-1).astype(x.dtype)


def to_blocks(a, nb):
    return a.reshape((a.shape[0], nb, Q_BLOCK) + a.shape[2:]).swapaxes(0, 1)


def from_blocks(a):
    a = a.swapaxes(0, 1)
    return a.reshape((a.shape[0], a.shape[1] * a.shape[2]) + a.shape[3:])


def dsa_attention(q, k, v, q_idx, k_idx, w_idx):
    B, S, H, D = q.shape
    nb = S // Q_BLOCK
    top_k = min(TOPK_MAX, S // 4)
    scale = D ** -0.5
    kpos = jnp.arange(S)
    gather = jax.vmap(lambda t, i: t[i])

    def one_block(args):
        qb, qib, wb, i = args
        qpos = i * Q_BLOCK + jnp.arange(Q_BLOCK)
        logits = jnp.einsum('bqhd,bkd->bqhk', qib, k_idx, preferred_element_type=jnp.float32) * (IDX_DIM ** -0.5)
        score = jnp.einsum('bqh,bqhk->bqk', wb.astype(jnp.float32) * (IDX_HEADS ** -0.5), jax.nn.relu(logits))
        causal = kpos[None, :] <= qpos[:, None]
        score = jnp.where(causal[None], score, -jnp.inf)
        _, sel = lax.top_k(score, top_k)
        valid = sel <= qpos[None, :, None]
        k_sel = gather(k, sel)
        v_sel = gather(v, sel)
        s = jnp.einsum('bqhd,bqkd->bqhk', qb, k_sel, preferred_element_type=jnp.float32) * scale
        s = jnp.where(valid[:, :, None, :], s, -jnp.inf)
        pr = jax.nn.softmax(s, axis=-1)
        return jnp.einsum('bqhk,bqkd->bqhd', pr.astype(v.dtype), v_sel)

    out = lax.map(one_block, (to_blocks(q, nb), to_blocks(q_idx, nb), to_blocks(w_idx, nb), jnp.arange(nb)))
    return from_blocks(out)


def causal_attention(q, k, v):
    B, S, H, Dq = q.shape
    nb = S // Q_BLOCK
    scale = Dq ** -0.5
    kpos = jnp.arange(S)

    def one_block(args):
        qb, i = args
        qpos = i * Q_BLOCK + jnp.arange(Q_BLOCK)
        s = jnp.einsum('bqhd,bkhd->bhqk', qb, k, preferred_element_type=jnp.float32) * scale
        s = jnp.where((kpos[None, :] <= qpos[:, None])[None, None], s, -jnp.inf)
        pr = jax.nn.softmax(s, axis=-1)
        return jnp.einsum('bhqk,bkhd->bqhd', pr.astype(v.dtype), v)

    out = lax.map(one_block, (to_blocks(q, nb), jnp.arange(nb)))
    return from_blocks(out)


def conformer_conv(u, conv_w, conv_b, ln_g, ln_b):
    a, g = jnp.split(u, 2, axis=-1)
    h = a * jax.nn.sigmoid(g)
    h = lax.conv_general_dilated(h, conv_w[:, None, :].astype(h.dtype), window_strides=(1,),
                                 padding=((CONV_W - 1, 0),),
                                 dimension_numbers=('NWC', 'WIO', 'NWC'),
                                 feature_group_count=C_CH) + conv_b
    return jax.nn.silu(layer_norm(h, ln_g, ln_b))


def hybrid_layer(x, p_i, rope_a, rope_b, w_in, b_gate, q_norm_g, w_uq, kv_norm_g, w_ukv,
                 conv_w, conv_b, conv_ln_g, conv_ln_b, w_branch, w_out, ln1_g, ln1_b,
                 w_ffn_in, w_ffn_out, w_ple_gate, w_ple_proj, ln2_g, ln2_b):
    B, S, _ = x.shape
    cos_a, sin_a = rope_a
    cos_b, sin_b = rope_b
    h = x @ w_in
    qa, ka, va, qi, ki, wi, cq, ckv, kr, conv_in, gate_pre = jnp.split(h, IN_SPLITS, axis=-1)

    qa = apply_rope(qa.reshape(B, S, A_HEADS, A_HEAD_DIM), cos_a, sin_a)
    ka = apply_rope(ka, cos_a, sin_a)
    qi = apply_rope(qi.reshape(B, S, IDX_HEADS, IDX_DIM), cos_a, sin_a)
    ki = apply_rope(ki, cos_a, sin_a)
    y_a = dsa_attention(qa, ka, va, qi, ki, wi).reshape(B, S, BRANCH_W)

    qb = (rms_norm(cq, q_norm_g) @ w_uq).reshape(B, S, B_HEADS, B_NOPE + B_ROPE)
    q_nope, q_rope = jnp.split(qb, [B_NOPE], axis=-1)
    q_rope = apply_rope(q_rope, cos_b, sin_b)
    kvb = (rms_norm(ckv, kv_norm_g) @ w_ukv).reshape(B, S, B_HEADS, B_NOPE + B_V)
    k_nope, v_b = jnp.split(kvb, [B_NOPE], axis=-1)
    k_rope = apply_rope(kr, cos_b, sin_b)
    q_full = jnp.concatenate([q_nope, q_rope], axis=-1)
    k_full = jnp.concatenate([k_nope, jnp.broadcast_to(k_rope[:, :, None, :], (B, S, B_HEADS, B_ROPE))], axis=-1)
    y_b = causal_attention(q_full, k_full, v_b).reshape(B, S, BRANCH_W)

    y_c = conformer_conv(conv_in, conv_w, conv_b, conv_ln_g, conv_ln_b)

    gates = jax.nn.sigmoid(gate_pre.reshape(B, S, N_BRANCH, D_MODEL) + b_gate)
    merged = (gates[:, :, 0] * (y_a @ w_branch[0])
              + gates[:, :, 1] * (y_b @ w_branch[1])
              + gates[:, :, 2] * (y_c @ w_branch[2]))
    x = layer_norm(ALPHA * x + merged @ w_out, ln1_g, ln1_b)

    f_gate, f_up = jnp.split(x @ w_ffn_in, 2, axis=-1)
    ffn = (jax.nn.silu(f_gate) * f_up) @ w_ffn_out
    ple = jax.nn.sigmoid(x @ w_ple_gate) * (p_i @ w_ple_proj)
    return layer_norm(ALPHA * x + ffn + ple, ln2_g, ln2_b)


def setup_inputs(seed: int = 0) -> dict:
    key = jax.random.key(seed)
    ks = jax.random.split(key, 24)
    f32 = jnp.float32
    nrm = lambda k, shp, sc: jax.random.normal(k, shp, f32) * sc
    L, D = DEPTH, D_MODEL
    offset = jax.random.randint(ks[2], (BATCH, 1), 0, SEQ, dtype=jnp.int32)
    return {
        "x": nrm(ks[0], (BATCH, SEQ, D), 1.0),
        "p": nrm(ks[1], (DEPTH, BATCH, SEQ, PLE_DIM), 1.0),
        "positions": offset + jnp.arange(SEQ, dtype=jnp.int32)[None, :],
        "w_in": nrm(ks[3], (L, D, N_IN), D ** -0.5),
        "b_gate": nrm(ks[4], (L, N_BRANCH, D), 0.02),
        "q_norm_g": 1.0 + nrm(ks[5], (L, Q_LORA), 0.02),
        "w_uq": nrm(ks[6], (L, Q_LORA, B_HEADS * (B_NOPE + B_ROPE)), Q_LORA ** -0.5),
        "kv_norm_g": 1.0 + nrm(ks[7], (L, KV_LORA), 0.02),
        "w_ukv": nrm(ks[8], (L, KV_LORA, B_HEADS * (B_NOPE + B_V)), KV_LORA ** -0.5),
        "conv_w": nrm(ks[9], (L, CONV_W, C_CH), CONV_W ** -0.5),
        "conv_b": nrm(ks[10], (L, C_CH), 0.02),
        "conv_ln_g": 1.0 + nrm(ks[11], (L, C_CH), 0.02),
        "conv_ln_b": nrm(ks[12], (L, C_CH), 0.02),
        "w_branch": nrm(ks[13], (L, N_BRANCH, BRANCH_W, D), BRANCH_W ** -0.5 * OUT_SCALE),
        "w_out": nrm(ks[14], (L, D, D), D ** -0.5 * OUT_SCALE),
        "ln1_g": 1.0 + nrm(ks[15], (L, D), 0.02),
        "ln1_b": nrm(ks[16], (L, D), 0.02),
        "w_ffn_in": nrm(ks[17], (L, D, 2 * FF), D ** -0.5),
        "w_ffn_out": nrm(ks[18], (L, FF, D), FF ** -0.5 * OUT_SCALE),
        "w_ple_gate": nrm(ks[19], (L, D, D), D ** -0.5),
        "w_ple_proj": nrm(ks[20], (L, PLE_DIM, D), PLE_DIM ** -0.5 * OUT_SCALE),
        "ln2_g": 1.0 + nrm(ks[21], (L, D), 0.02),
        "ln2_b": nrm(ks[22], (L, D), 0.02),
    }


def reference(x, p, positions, w_in, b_gate, q_norm_g, w_uq, kv_norm_g, w_ukv, conv_w, conv_b,
              conv_ln_g, conv_ln_b, w_branch, w_out, ln1_g, ln1_b, w_ffn_in, w_ffn_out,
              w_ple_gate, w_ple_proj, ln2_g, ln2_b):
    rope_a = rope_tables(positions, A_HEAD_DIM)
    rope_b = rope_tables(positions, B_ROPE)
    for i in range(DEPTH):
        x = hybrid_layer(x, p[i], rope_a, rope_b, w_in[i], b_gate[i], q_norm_g[i], w_uq[i],
                         kv_norm_g[i], w_ukv[i], conv_w[i], conv_b[i], conv_ln_g[i], conv_ln_b[i],
                         w_branch[i], w_out[i], ln1_g[i], ln1_b[i], w_ffn_in[i], w_ffn_out[i],
                         w_ple_gate[i], w_ple_proj[i], ln2_g[i], ln2_b[i])
    return x
```

```python
import functools
import math

import jax
import jax.numpy as jnp
from jax import lax
from jax.experimental import pallas as pl
from jax.experimental.pallas import tpu as pltpu

F32 = jnp.float32
BF16 = jnp.bfloat16
I32 = jnp.int32

LANES = 128

PLE_DIM = 256
A_HEADS = 8
A_HEAD_DIM = 64
IDX_HEADS = 4
IDX_DIM = 64
TOPK_MAX = 256
B_HEADS = 8
B_NOPE = 64
B_ROPE = 32
B_V = 64
Q_LORA = 384
KV_LORA = 256
C_CH = 512
CONV_W = 31
N_BRANCH = 3
ROPE_THETA = 10000.0
LN_EPS = 1e-5
RMS_EPS = 1e-6

INT_MIN = -(2 ** 31)
NEG_BIG = -1e30

VMEM_LIMIT = 56 * 1024 * 1024


def _cparams(sem):
    return pltpu.CompilerParams(dimension_semantics=sem, vmem_limit_bytes=VMEM_LIMIT)


def _layer_norm(v, g, b):
    mu = jnp.mean(v, axis=-1, keepdims=True)
    d = v - mu
    var = jnp.mean(d * d, axis=-1, keepdims=True)
    return d * lax.rsqrt(var + LN_EPS) * g + b


def _sigmoid(v):
    return 1.0 / (1.0 + jnp.exp(-v))


def _rope_group(h, c, sa, sb, half):
    return h * c + pltpu.roll(h, LANES - half, 1) * sa + pltpu.roll(h, half, 1) * sb


def _proj_a_kernel(x_ref, w_ref, c_ref, sa_ref, sb_ref, qa_ref, qi_ref, kk_ref, va_ref, wi_ref):
    xb = x_ref[...].astype(BF16)
    h = jnp.dot(xb, w_ref[...], preferred_element_type=F32)
    c, sa, sb = c_ref[...], sa_ref[...], sb_ref[...]
    half = A_HEAD_DIM // 2
    for g in range(A_HEADS):
        hg = h[:, g * LANES:(g + 1) * LANES]
        qa_ref[:, g * LANES:(g + 1) * LANES] = _rope_group(hg, c, sa, sb, half).astype(BF16)
    off = A_HEADS
    for g in range(IDX_HEADS):
        hg = h[:, (off + g) * LANES:(off + g + 1) * LANES]
        qi_ref[:, g * LANES:(g + 1) * LANES] = _rope_group(hg, c, sa, sb, half).astype(BF16)
    off += IDX_HEADS
    hg = h[:, off * LANES:(off + 1) * LANES]
    kk_ref[...] = _rope_group(hg, c, sa, sb, half).astype(BF16)
    off += 1
    va_ref[...] = h[:, off * LANES:off * LANES + A_HEAD_DIM].astype(BF16)
    off += 1
    wi_ref[...] = h[:, off * LANES:(off + 1) * LANES]


def _proj_a(x2, w_a, tabs, tm):
    T, D = x2.shape
    n = w_a.shape[1]
    row = lambda w: pl.BlockSpec((tm, w), lambda i: (i, 0))
    return pl.pallas_call(
        _proj_a_kernel,
        grid=(T // tm,),
        in_specs=[row(D), pl.BlockSpec((D, n), lambda i: (0, 0)), row(LANES), row(LANES), row(LANES)],
        out_specs=[row(A_HEADS * LANES), row(IDX_HEADS * LANES), row(LANES), row(A_HEAD_DIM), row(LANES)],
        out_shape=[jax.ShapeDtypeStruct((T, A_HEADS * LANES), BF16),
                   jax.ShapeDtypeStruct((T, IDX_HEADS * LANES), BF16),
                   jax.ShapeDtypeStruct((T, LANES), BF16),
                   jax.ShapeDtypeStruct((T, A_HEAD_DIM), BF16),
                   jax.ShapeDtypeStruct((T, LANES), F32)],
        compiler_params=_cparams(("parallel",)),
        name="proj_a",
    )(x2, w_a, *tabs)


def _proj_b_kernel(x_ref, w_ref, gq_ref, gkv_ref, wuq_ref, wukv_ref, c_ref, sa_ref, sb_ref,
                   q_ref, k_ref, v_ref, *, q_scale):
    xb = x_ref[...].astype(BF16)
    h = jnp.dot(xb, w_ref[...], preferred_element_type=F32)
    c, sa, sb = c_ref[...], sa_ref[...], sb_ref[...]
    half = B_ROPE // 2
    cq = h[:, :Q_LORA]
    cqn = cq * lax.rsqrt(jnp.mean(cq * cq, axis=-1, keepdims=True) + RMS_EPS) * gq_ref[...]
    ckv = h[:, Q_LORA:Q_LORA + KV_LORA]
    ckvn = ckv * lax.rsqrt(jnp.mean(ckv * ckv, axis=-1, keepdims=True) + RMS_EPS) * gkv_ref[...]
    kr = _rope_group(h[:, Q_LORA + KV_LORA:], c, sa, sb, half)
    q = jnp.dot(cqn.astype(BF16), wuq_ref[...], preferred_element_type=F32)
    kv = jnp.dot(ckvn.astype(BF16), wukv_ref[...], preferred_element_type=F32)
    for g in range(B_HEADS):
        qg = q[:, g * LANES:(g + 1) * LANES]
        q_ref[:, g * LANES:(g + 1) * LANES] = (_rope_group(qg, c, sa, sb, half) * q_scale).astype(BF16)
        k_ref[:, g * LANES:(g + 1) * LANES] = (kv[:, g * LANES:(g + 1) * LANES] + kr).astype(BF16)
    v_ref[...] = kv[:, B_HEADS * LANES:].astype(BF16)


def _proj_b(x2, w_b, gq, gkv, wuq, wukv, tabs, tm):
    T, D = x2.shape
    row = lambda w: pl.BlockSpec((tm, w), lambda i: (i, 0))
    full = lambda a: pl.BlockSpec(a.shape, lambda i: (0, 0))
    q_scale = float((B_NOPE + B_ROPE) ** -0.5)
    return pl.pallas_call(
        functools.partial(_proj_b_kernel, q_scale=q_scale),
        grid=(T // tm,),
        in_specs=[row(D), full(w_b), full(gq), full(gkv), full(wuq), full(wukv),
                  row(LANES), row(LANES), row(LANES)],
        out_specs=[row(B_HEADS * LANES), row(B_HEADS * LANES), row(B_HEADS * B_V)],
        out_shape=[jax.ShapeDtypeStruct((T, B_HEADS * LANES), BF16),
                   jax.ShapeDtypeStruct((T, B_HEADS * LANES), BF16),
                   jax.ShapeDtypeStruct((T, B_HEADS * B_V), BF16)],
        compiler_params=_cparams(("parallel",)),
        name="proj_b",
    )(x2, w_b, gq, gkv, wuq, wukv, *tabs)


CONV_HALO = 32
CONV_ROWS = 64


def _conv_kernel(x_ref, w_ref, cw_ref, cb_ref, g_ref, b_ref, o_ref, hbuf, *, tm):
    j = pl.program_id(1)

    @pl.when(j == 0)
    def _():
        hbuf[0:CONV_HALO, :] = jnp.zeros((CONV_HALO, C_CH), F32)

    @pl.when(j > 0)
    def _():
        hbuf[0:CONV_HALO, :] = hbuf[tm:tm + CONV_HALO, :]

    xb = x_ref[...].astype(BF16)
    u = jnp.dot(xb, w_ref[...], preferred_element_type=F32)
    hbuf[CONV_HALO:CONV_HALO + tm, :] = u[:, :C_CH] * _sigmoid(u[:, C_CH:])
    first = CONV_HALO - (CONV_W - 1)
    for sb in range(tm // CONV_ROWS):
        r0 = sb * CONV_ROWS
        acc = jnp.zeros((CONV_ROWS, C_CH), F32) + cb_ref[...]
        for t in range(CONV_W):
            s0 = r0 + first + t
            acc = acc + hbuf[s0:s0 + CONV_ROWS, :] * cw_ref[t:t + 1, :]
        y = _layer_norm(acc, g_ref[...], b_ref[...])
        o_ref[r0:r0 + CONV_ROWS, :] = (y * _sigmoid(y)).astype(BF16)


def _conv_branch(x2, w_c, cw, cb, g, b, batch, seq, tm):
    T, D = x2.shape
    nj = seq // tm
    row = lambda w: pl.BlockSpec((tm, w), lambda bi, j: (bi * nj + j, 0))
    full = lambda a: pl.BlockSpec(a.shape, lambda bi, j: (0, 0))
    return pl.pallas_call(
        functools.partial(_conv_kernel, tm=tm),
        grid=(batch, nj),
        in_specs=[row(D), full(w_c), full(cw), full(cb), full(g), full(b)],
        out_specs=row(C_CH),
        out_shape=jax.ShapeDtypeStruct((T, C_CH), BF16),
        scratch_shapes=[pltpu.VMEM((tm + CONV_HALO, C_CH), F32)],
        compiler_params=_cparams(("arbitrary", "arbitrary")),
        name="conv_branch",
    )(x2, w_c, cw, cb, g, b)


def _dsa_kernel(qa_ref, qi_ref, wi_ref, kk_ref, va_ref, o_ref, keys_ref, m_ref, l_ref, acc_ref,
                *, tq, kc, top_k, idx_bits):
    i = pl.program_id(1)
    nkc = ((i + 1) * tq + kc - 1) // kc
    int_min = jnp.int32(INT_MIN)
    kthf = jnp.float32(top_k)
    nt = (((1,), (1,)), ((), ()))

    qi = qi_ref[...]
    qi_st = jnp.concatenate([qi[:, h * LANES:(h + 1) * LANES] for h in range(IDX_HEADS)], axis=0)
    w = wi_ref[...]
    wcols = [w[:, h:h + 1] for h in range(IDX_HEADS)]
    q_pos = i * tq + lax.broadcasted_iota(I32, (tq, kc), 0)
    k_iota = lax.broadcasted_iota(I32, (tq, kc), 1)

    def score_body(c, carry):
        k0 = pl.multiple_of(c * kc, kc)
        kblk = kk_ref[pl.ds(k0, kc), :]
        lg = lax.dot_general(qi_st, kblk, nt, preferred_element_type=F32)
        sc = wcols[0] * jnp.maximum(lg[0:tq], 0.0)
        for h in range(1, IDX_HEADS):
            sc = sc + wcols[h] * jnp.maximum(lg[h * tq:(h + 1) * tq], 0.0)
        bits = lax.bitcast_convert_type(sc, I32)
        key = jnp.where(bits < 0, int_min - bits, bits)
        keys_ref[c] = jnp.where(k0 + k_iota <= q_pos, key, int_min)
        return carry

    lax.fori_loop(0, nkc, score_body, 0)

    def count(pred):
        def body(c, acc):
            m = jnp.where(pred(c, keys_ref[c]), 1.0, 0.0)
            part = m[:, 0:LANES]
            for jj in range(1, kc // LANES):
                part = part + m[:, jj * LANES:(jj + 1) * LANES]
            return acc + part
        acc = lax.fori_loop(0, nkc, body, jnp.zeros((tq, LANES), F32))
        return jnp.sum(acc, axis=1, keepdims=True)

    def count_ge(cand):
        return count(lambda c, kb: kb >= cand)

    thr0 = jnp.where(count_ge(jnp.zeros((tq, 1), I32)) >= kthf, jnp.int32(0), int_min)

    def bit_body(b, thr):
        cand = thr + lax.shift_left(jnp.int32(1), jnp.int32(30) - b)
        return jnp.where(count_ge(cand) >= kthf, cand, thr)

    thr = lax.fori_loop(0, 31, bit_body, thr0)

    need = jnp.logical_and(count_ge(thr) > kthf, thr > int_min)
    any_need = jnp.max(jnp.where(need, 1.0, 0.0)) > 0.0

    @pl.when(any_need)
    def _():
        r = kthf - count_ge(thr + 1)

        def jbody(b, j0):
            cand = j0 + lax.shift_left(jnp.int32(1), jnp.int32(idx_bits - 1) - b)
            cnt = count(lambda c, kb: jnp.logical_and(kb == thr, c * kc + k_iota < cand))
            return jnp.where(cnt < r, cand, j0)

        j0 = lax.fori_loop(0, idx_bits, jbody, jnp.zeros((tq, 1), I32))

        def drop_body(c, carry):
            kb = keys_ref[c]
            drop = jnp.logical_and(jnp.logical_and(need, kb == thr), c * kc + k_iota > j0)
            keys_ref[c] = jnp.where(drop, int_min, kb)
            return carry

        lax.fori_loop(0, nkc, drop_body, 0)

    thr_sel = jnp.maximum(thr, int_min + 1)

    q = qa_ref[...]
    q_st = jnp.concatenate([q[:, h * LANES:(h + 1) * LANES] for h in range(A_HEADS)], axis=0)
    m_ref[...] = jnp.full(m_ref.shape, NEG_BIG, F32)
    l_ref[...] = jnp.zeros(l_ref.shape, F32)
    acc_ref[...] = jnp.zeros(acc_ref.shape, F32)

    def att_body(c, carry):
        k0 = pl.multiple_of(c * kc, kc)
        kblk = kk_ref[pl.ds(k0, kc), :]
        vblk = va_ref[pl.ds(k0, kc), :]
        s = lax.dot_general(q_st, kblk, nt, preferred_element_type=F32)
        bias = jnp.where(keys_ref[c] >= thr_sel, 0.0, NEG_BIG)
        s = (s.reshape(A_HEADS, tq, kc) + bias[None]).reshape(A_HEADS * tq, kc)
        m_prev = m_ref[...]
        m_new = jnp.maximum(m_prev, jnp.max(s, axis=1, keepdims=True))
        alpha = jnp.exp(m_prev - m_new)
        p = jnp.exp(s - m_new)
        l_ref[...] = alpha * l_ref[...] + jnp.sum(p, axis=1, keepdims=True)
        acc_ref[...] = alpha * acc_ref[...] + jnp.dot(p.astype(BF16), vblk, preferred_element_type=F32)
        m_ref[...] = m_new
        return carry

    lax.fori_loop(0, nkc, att_body, 0)
    out = acc_ref[...] / l_ref[...]
    for h in range(A_HEADS):
        o_ref[:, h * A_HEAD_DIM:(h + 1) * A_HEAD_DIM] = out[h * tq:(h + 1) * tq].astype(BF16)


def _dsa_attention(qa, qi, wi, kk, va, batch, seq, tq, kc):
    T = qa.shape[0]
    nq = seq // tq
    top_k = min(TOPK_MAX, seq // 4)
    qrow = lambda w: pl.BlockSpec((tq, w), lambda b, i: (b * nq + i, 0))
    krow = lambda w: pl.BlockSpec((seq, w), lambda b, i: (b, 0))
    return pl.pallas_call(
        functools.partial(_dsa_kernel, tq=tq, kc=kc, top_k=top_k, idx_bits=(seq - 1).bit_length()),
        grid=(batch, nq),
        in_specs=[qrow(A_HEADS * LANES), qrow(IDX_HEADS * LANES), qrow(LANES), krow(LANES), krow(A_HEAD_DIM)],
        out_specs=qrow(A_HEADS * A_HEAD_DIM),
        out_shape=jax.ShapeDtypeStruct((T, A_HEADS * A_HEAD_DIM), BF16),
        scratch_shapes=[pltpu.VMEM((seq // kc, tq, kc), I32),
                        pltpu.VMEM((A_HEADS * tq, 1), F32),
                        pltpu.VMEM((A_HEADS * tq, 1), F32),
                        pltpu.VMEM((A_HEADS * tq, A_HEAD_DIM), F32)],
        compiler_params=_cparams(("parallel", "arbitrary")),
        name="dsa_attention",
    )(qa, qi, wi, kk, va)


def _mla_kernel(q_ref, k_ref, v_ref, o_ref, m_ref, l_ref, acc_ref, *, tq):
    i = pl.program_id(2)
    nt = (((1,), (1,)), ((), ()))
    q_pos = lax.broadcasted_iota(I32, (tq, tq), 0)
    k_pos = lax.broadcasted_iota(I32, (tq, tq), 1)
    outs = []
    for hh in range(2):
        q = q_ref[:, hh * LANES:(hh + 1) * LANES]
        m_ref[...] = jnp.full(m_ref.shape, NEG_BIG, F32)
        l_ref[...] = jnp.zeros(l_ref.shape, F32)
        acc_ref[...] = jnp.zeros(acc_ref.shape, F32)

        def step(c, masked):
            k0 = pl.multiple_of(c * tq, tq)
            kblk = k_ref[pl.ds(k0, tq), hh * LANES:(hh + 1) * LANES]
            vblk = v_ref[pl.ds(k0, tq), :]
            s = lax.dot_general(q, kblk, nt, preferred_element_type=F32)
            if masked:
                s = jnp.where(k_pos <= q_pos, s, NEG_BIG)
            m_prev = m_ref[...]
            m_new = jnp.maximum(m_prev, jnp.max(s, axis=1, keepdims=True))
            alpha = jnp.exp(m_prev - m_new)
            p = jnp.exp(s - m_new)
            l_ref[...] = alpha * l_ref[...] + jnp.sum(p, axis=1, keepdims=True)
            acc_ref[...] = alpha * acc_ref[...] + jnp.dot(p.astype(BF16), vblk, preferred_element_type=F32)
            m_ref[...] = m_new

        def body(c, carry):
            step(c, False)
            return carry

        lax.fori_loop(0, i, body, 0)
        step(i, True)
        outs.append(acc_ref[...] / l_ref[...])
    lane = lax.broadcasted_iota(I32, (tq, LANES), 1)
    o_ref[...] = jnp.where(lane < B_V, outs[0], outs[1]).astype(BF16)


def _mla_attention(q, k, v, batch, seq, tq):
    T = q.shape[0]
    nq = seq // tq
    pairs = B_HEADS // 2
    return pl.pallas_call(
        functools.partial(_mla_kernel, tq=tq),
        grid=(batch, pairs, nq),
        in_specs=[pl.BlockSpec((tq, 2 * LANES), lambda b, h, i: (b * nq + i, h)),
                  pl.BlockSpec((seq, 2 * LANES), lambda b, h, i: (b, h)),
                  pl.BlockSpec((seq, 2 * B_V), lambda b, h, i: (b, h))],
        out_specs=pl.BlockSpec((tq, 2 * B_V), lambda b, h, i: (b * nq + i, h)),
        out_shape=jax.ShapeDtypeStruct((T, B_HEADS * B_V), BF16),
        scratch_shapes=[pltpu.VMEM((tq, 1), F32), pltpu.VMEM((tq, 1), F32), pltpu.VMEM((tq, 2 * B_V), F32)],
        compiler_params=_cparams(("parallel", "parallel", "arbitrary")),
        name="mla_attention",
    )(q, k, v)


def _merge_kernel(x_ref, ya_ref, yb_ref, yc_ref, wg_ref, bg_ref, wbr_ref, wo_ref, g_ref, b_ref, o_ref,
                  *, alpha, d):
    x = x_ref[...]
    xb = x.astype(BF16)
    merged = None
    for k, y_ref in enumerate((ya_ref, yb_ref, yc_ref)):
        pre = jnp.dot(xb, wg_ref[:, k * d:(k + 1) * d], preferred_element_type=F32) + bg_ref[k:k + 1, :]
        br = jnp.dot(y_ref[...], wbr_ref[k], preferred_element_type=F32)
        term = _sigmoid(pre) * br
        merged = term if merged is None else merged + term
    mix = jnp.dot(merged.astype(BF16), wo_ref[...], preferred_element_type=F32)
    o_ref[...] = _layer_norm(alpha * x + mix, g_ref[...], b_ref[...])


def _merge(x2, ya, yb, yc, wg, bg, wbr, wo, g, b, alpha, tm):
    T, D = x2.shape
    row = lambda w: pl.BlockSpec((tm, w), lambda i: (i, 0))
    full = lambda a: pl.BlockSpec(a.shape, lambda i: (0,) * a.ndim)
    return pl.pallas_call(
        functools.partial(_merge_kernel, alpha=alpha, d=D),
        grid=(T // tm,),
        in_specs=[row(D), row(ya.shape[1]), row(yb.shape[1]), row(yc.shape[1]),
                  full(wg), full(bg), full(wbr), full(wo), full(g), full(b)],
        out_specs=row(D),
        out_shape=jax.ShapeDtypeStruct((T, D), F32),
        compiler_params=_cparams(("parallel",)),
        name="merge_ln1",
    )(x2, ya, yb, yc, wg, bg, wbr, wo, g, b)


def _ffn_kernel(x_ref, p_ref, wg_ref, wu_ref, wo_ref, wpg_ref, wpp_ref, g_ref, b_ref, o_ref, acc_ref,
                *, alpha):
    j = pl.program_id(1)
    xb = x_ref[...].astype(BF16)

    @pl.when(j == 0)
    def _():
        gate = _sigmoid(jnp.dot(xb, wpg_ref[...], preferred_element_type=F32))
        emb = jnp.dot(p_ref[...].astype(BF16), wpp_ref[...], preferred_element_type=F32)
        acc_ref[...] = alpha * x_ref[...] + gate * emb

    fg = jnp.dot(xb, wg_ref[...], preferred_element_type=F32)
    fu = jnp.dot(xb, wu_ref[...], preferred_element_type=F32)
    a = (fg * _sigmoid(fg) * fu).astype(BF16)
    acc_ref[...] += jnp.dot(a, wo_ref[...], preferred_element_type=F32)

    @pl.when(j == pl.num_programs(1) - 1)
    def _():
        o_ref[...] = _layer_norm(acc_ref[...], g_ref[...], b_ref[...])


def _ffn(x2, p2, wg, wu, wo, wpg, wpp, g, b, alpha, tm, tf):
    T, D = x2.shape
    ff = wg.shape[1]
    row = lambda w: pl.BlockSpec((tm, w), lambda i, j: (i, 0))
    full = lambda a: pl.BlockSpec(a.shape, lambda i, j: (0, 0))
    return pl.pallas_call(
        functools.partial(_ffn_kernel, alpha=alpha),
        grid=(T // tm, ff // tf),
        in_specs=[row(D), row(p2.shape[1]),
                  pl.BlockSpec((D, tf), lambda i, j: (0, j)),
                  pl.BlockSpec((D, tf), lambda i, j: (0, j)),
                  pl.BlockSpec((tf, D), lambda i, j: (j, 0)),
                  full(wpg), full(wpp), full(g), full(b)],
        out_specs=row(D),
        out_shape=jax.ShapeDtypeStruct((T, D), F32),
        scratch_shapes=[pltpu.VMEM((tm, D), F32)],
        compiler_params=_cparams(("parallel", "arbitrary")),
        name="ffn_ple_ln2",
    )(x2, p2, wg, wu, wo, wpg, wpp, g, b)


def _rope_tables(positions, dim, lane_lo, period):
    half = dim // 2
    inv = 1.0 / (ROPE_THETA ** (jnp.arange(0, dim, 2, dtype=F32) / dim))
    ang = positions.reshape(-1).astype(F32)[:, None] * inv
    cos, sin = jnp.cos(ang), jnp.sin(ang)
    lane = jnp.arange(LANES)
    rel = lane % period - lane_lo
    inside = (rel >= 0) & (rel < dim)
    f = jnp.where(inside, rel % half, 0)
    lo = inside & (rel < half)
    hi = inside & (rel >= half)
    c = jnp.where(inside[None], cos[:, f], 1.0)
    sa = jnp.where(lo[None], -sin[:, f], 0.0)
    sb = jnp.where(hi[None], sin[:, f], 0.0)
    return c, sa, sb


def _pad_cols(w, n):
    return jnp.pad(w, ((0, 0), (0, n - w.shape[1])))


def _prep_layer(w_in, w_uq, w_ukv, w_ffn_in):
    d = w_in.shape[0]
    sizes = (A_HEADS * A_HEAD_DIM, A_HEAD_DIM, A_HEAD_DIM, IDX_HEADS * IDX_DIM, IDX_DIM, IDX_HEADS,
             Q_LORA, KV_LORA, B_ROPE, 2 * C_CH, N_BRANCH * d)
    splits = [sum(sizes[:k + 1]) for k in range(len(sizes) - 1)]
    qa, ka, va, qi, ki, wi, cq, ckv, kr, conv, gate = jnp.split(w_in, splits, axis=1)
    qa = jnp.pad((qa * A_HEAD_DIM ** -0.5).reshape(d, A_HEADS, A_HEAD_DIM),
                 ((0, 0), (0, 0), (0, LANES - A_HEAD_DIM))).reshape(d, A_HEADS * LANES)
    qi = jnp.pad(qi.reshape(d, IDX_HEADS, IDX_DIM),
                 ((0, 0), (0, 0), (LANES - IDX_DIM, 0))).reshape(d, IDX_HEADS * LANES)
    wi = wi * (IDX_DIM ** -0.5 * IDX_HEADS ** -0.5)
    w_a = jnp.concatenate([qa, qi, ka, ki, _pad_cols(va, LANES), _pad_cols(wi, LANES)], axis=1)
    kr = jnp.pad(kr, ((0, 0), (B_NOPE, LANES - B_NOPE - B_ROPE)))
    w_b = jnp.concatenate([cq, ckv, kr], axis=1)
    wuq = jnp.pad(w_uq.reshape(-1, B_HEADS, B_NOPE + B_ROPE),
                  ((0, 0), (0, 0), (0, LANES - B_NOPE - B_ROPE))).reshape(-1, B_HEADS * LANES)
    ukv = w_ukv.reshape(-1, B_HEADS, B_NOPE + B_V)
    wuk = jnp.pad(ukv[:, :, :B_NOPE], ((0, 0), (0, 0), (0, LANES - B_NOPE))).reshape(-1, B_HEADS * LANES)
    wuv = ukv[:, :, B_NOPE:].reshape(-1, B_HEADS * B_V)
    wukv = jnp.concatenate([wuk, wuv], axis=1)
    ff = w_ffn_in.shape[1] // 2
    bf = lambda a: a.astype(BF16)
    return dict(w_a=bf(w_a), w_b=bf(w_b), w_c=bf(conv), w_g=bf(gate), wuq=bf(wuq), wukv=bf(wukv),
                wfg=bf(w_ffn_in[:, :ff]), wfu=bf(w_ffn_in[:, ff:]))


def _ffn_tile(ff):
    for tf in (512, 256, 128):
        if ff % tf == 0:
            return tf
    return ff


def kernel(x, p, positions, w_in, b_gate, q_norm_g, w_uq, kv_norm_g, w_ukv, conv_w, conv_b, conv_ln_g,
           conv_ln_b, w_branch, w_out, ln1_g, ln1_b, w_ffn_in, w_ffn_out, w_ple_gate, w_ple_proj, ln2_g, ln2_b):
    batch, seq, d = x.shape
    depth = w_in.shape[0]
    T = batch * seq
    alpha = float((2 * depth) ** 0.25)
    tm = min(512, seq)
    tq_a = min(128, seq)
    kc_a = min(512, seq)
    tq_b = min(512, seq)
    tabs_a = _rope_tables(positions, A_HEAD_DIM, 0, A_HEAD_DIM)
    tabs_b = _rope_tables(positions, B_ROPE, B_NOPE, LANES)
    row = lambda v: v.reshape(1, -1)
    x2 = x.reshape(T, d)
    for i in range(depth):
        w = _prep_layer(w_in[i], w_uq[i], w_ukv[i], w_ffn_in[i])
        qa, qi, kk, va, wi = _proj_a(x2, w["w_a"], tabs_a, tm)
        qb, kb, vb = _proj_b(x2, w["w_b"], row(q_norm_g[i]), row(kv_norm_g[i]), w["wuq"], w["wukv"], tabs_b, tm)
        y_c = _conv_branch(x2, w["w_c"], conv_w[i], row(conv_b[i]), row(conv_ln_g[i]), row(conv_ln_b[i]),
                           batch, seq, tm)
        y_a = _dsa_attention(qa, qi, wi, kk, va, batch, seq, tq_a, kc_a)
        y_b = _mla_attention(qb, kb, vb, batch, seq, tq_b)
        x2 = _merge(x2, y_a, y_b, y_c, w["w_g"], b_gate[i], w_branch[i].astype(BF16), w_out[i].astype(BF16),
                    row(ln1_g[i]), row(ln1_b[i]), alpha, min(256, seq))
        x2 = _ffn(x2, p[i].reshape(T, -1), w["wfg"], w["wfu"], w_ffn_out[i].astype(BF16),
                  w_ple_gate[i].astype(BF16), w_ple_proj[i].astype(BF16), row(ln2_g[i]), row(ln2_b[i]),
                  alpha, tm, _ffn_tile(w_ffn_in.shape[2] // 2))
    return x2.reshape(batch, seq, d)
```

```python
import functools

import jax
import jax.numpy as jnp
from jax import lax
from jax.experimental import pallas as pl
from jax.experimental.pallas import tpu as pltpu

F32 = jnp.float32
BF16 = jnp.bfloat16
I32 = jnp.int32

LANES = 128

PLE_DIM = 256
A_HEADS = 8
A_HEAD_DIM = 64
IDX_HEADS = 4
IDX_DIM = 64
TOPK_MAX = 256
B_HEADS = 8
B_NOPE = 64
B_ROPE = 32
B_V = 64
Q_LORA = 384
KV_LORA = 256
C_CH = 512
CONV_W = 31
N_BRANCH = 3
ROPE_THETA = 10000.0
LN_EPS = 1e-5
RMS_EPS = 1e-6

INT_MIN = -(2 ** 31)
NEG_BIG = -1e30

VMEM_LIMIT = 56 * 1024 * 1024


def _cparams(sem):
    return pltpu.CompilerParams(dimension_semantics=sem, vmem_limit_bytes=VMEM_LIMIT)


def _layer_norm(v, g, b):
    mu = jnp.mean(v, axis=-1, keepdims=True)
    d = v - mu
    var = jnp.mean(d * d, axis=-1, keepdims=True)
    return d * lax.rsqrt(var + LN_EPS) * g + b


def _sigmoid(v):
    return 1.0 / (1.0 + jnp.exp(-v))


def _tree(op, parts):
    while len(parts) > 1:
        parts = [op(a, b) for a, b in zip(parts[::2], parts[1::2])] + ([parts[-1]] if len(parts) % 2 else [])
    return parts[0]


def _reduce_rows(op, v):
    slab = _tree(op, [v[r:r + 8] for r in range(0, v.shape[0], 8)])
    return op.reduce(slab, axis=0, keepdims=True)


def _rope_group(h, c, sa, sb, half):
    return h * c + pltpu.roll(h, LANES - half, 1) * sa + pltpu.roll(h, half, 1) * sb


def _proj_a_kernel(x_ref, w_ref, c_ref, sa_ref, sb_ref, qa_ref, qi_ref, kk_ref, va_ref, wi_ref):
    xb = x_ref[...].astype(BF16)
    h = jnp.dot(xb, w_ref[...], preferred_element_type=F32)
    c, sa, sb = c_ref[...], sa_ref[...], sb_ref[...]
    half = A_HEAD_DIM // 2
    for g in range(A_HEADS):
        hg = h[:, g * LANES:(g + 1) * LANES]
        qa_ref[:, g * LANES:(g + 1) * LANES] = _rope_group(hg, c, sa, sb, half).astype(BF16)
    off = A_HEADS
    for g in range(IDX_HEADS):
        hg = h[:, (off + g) * LANES:(off + g + 1) * LANES]
        qi_ref[:, g * LANES:(g + 1) * LANES] = _rope_group(hg, c, sa, sb, half).astype(BF16)
    off += IDX_HEADS
    hg = h[:, off * LANES:(off + 1) * LANES]
    kk_ref[...] = _rope_group(hg, c, sa, sb, half).astype(BF16)
    off += 1
    va_ref[0] = h[:, off * LANES:(off + 1) * LANES].T[:A_HEAD_DIM, :].astype(BF16)
    off += 1
    wi_ref[0] = h[:, off * LANES:(off + 1) * LANES].T[:8, :]


def _proj_a(x2, w_a, tabs, tm):
    T, D = x2.shape
    n = w_a.shape[1]
    row = lambda w: pl.BlockSpec((tm, w), lambda i: (i, 0))
    return pl.pallas_call(
        _proj_a_kernel,
        grid=(T // tm,),
        in_specs=[row(D), pl.BlockSpec((D, n), lambda i: (0, 0)), row(LANES), row(LANES), row(LANES)],
        out_specs=[row(A_HEADS * LANES), row(IDX_HEADS * LANES), row(LANES),
                   pl.BlockSpec((1, A_HEAD_DIM, tm), lambda i: (i, 0, 0)),
                   pl.BlockSpec((1, 8, tm), lambda i: (i, 0, 0))],
        out_shape=[jax.ShapeDtypeStruct((T, A_HEADS * LANES), BF16),
                   jax.ShapeDtypeStruct((T, IDX_HEADS * LANES), BF16),
                   jax.ShapeDtypeStruct((T, LANES), BF16),
                   jax.ShapeDtypeStruct((T // tm, A_HEAD_DIM, tm), BF16),
                   jax.ShapeDtypeStruct((T // tm, 8, tm), F32)],
        compiler_params=_cparams(("parallel",)),
        name="proj_a",
    )(x2, w_a, *tabs)


def _proj_b_kernel(x_ref, w_ref, gq_ref, gkv_ref, wuq_ref, wukv_ref, c_ref, sa_ref, sb_ref,
                   q_ref, k_ref, v_ref, *, q_scale):
    xb = x_ref[...].astype(BF16)
    h = jnp.dot(xb, w_ref[...], preferred_element_type=F32)
    c, sa, sb = c_ref[...], sa_ref[...], sb_ref[...]
    half = B_ROPE // 2
    cq = h[:, :Q_LORA]
    cqn = cq * lax.rsqrt(jnp.mean(cq * cq, axis=-1, keepdims=True) + RMS_EPS) * gq_ref[...]
    ckv = h[:, Q_LORA:Q_LORA + KV_LORA]
    ckvn = ckv * lax.rsqrt(jnp.mean(ckv * ckv, axis=-1, keepdims=True) + RMS_EPS) * gkv_ref[...]
    kr = _rope_group(h[:, Q_LORA + KV_LORA:], c, sa, sb, half)
    q = jnp.dot(cqn.astype(BF16), wuq_ref[...], preferred_element_type=F32)
    kv = jnp.dot(ckvn.astype(BF16), wukv_ref[...], preferred_element_type=F32)
    for g in range(B_HEADS):
        qg = q[:, g * LANES:(g + 1) * LANES]
        q_ref[:, g * LANES:(g + 1) * LANES] = (_rope_group(qg, c, sa, sb, half) * q_scale).astype(BF16)
        k_ref[:, g * LANES:(g + 1) * LANES] = (kv[:, g * LANES:(g + 1) * LANES] + kr).astype(BF16)
    v_ref[0] = kv[:, B_HEADS * LANES:].T.astype(BF16)


def _proj_b(x2, w_b, gq, gkv, wuq, wukv, tabs, tm):
    T, D = x2.shape
    row = lambda w: pl.BlockSpec((tm, w), lambda i: (i, 0))
    full = lambda a: pl.BlockSpec(a.shape, lambda i: (0, 0))
    q_scale = float((B_NOPE + B_ROPE) ** -0.5)
    return pl.pallas_call(
        functools.partial(_proj_b_kernel, q_scale=q_scale),
        grid=(T // tm,),
        in_specs=[row(D), full(w_b), full(gq), full(gkv), full(wuq), full(wukv),
                  row(LANES), row(LANES), row(LANES)],
        out_specs=[row(B_HEADS * LANES), row(B_HEADS * LANES),
                   pl.BlockSpec((1, B_HEADS * B_V, tm), lambda i: (i, 0, 0))],
        out_shape=[jax.ShapeDtypeStruct((T, B_HEADS * LANES), BF16),
                   jax.ShapeDtypeStruct((T, B_HEADS * LANES), BF16),
                   jax.ShapeDtypeStruct((T // tm, B_HEADS * B_V, tm), BF16)],
        compiler_params=_cparams(("parallel",)),
        name="proj_b",
    )(x2, w_b, gq, gkv, wuq, wukv, *tabs)


CONV_HALO = 32
CONV_ROWS = 64


def _conv_kernel(x_ref, w_ref, cw_ref, cb_ref, g_ref, b_ref, o_ref, hbuf, *, tm):
    j = pl.program_id(1)

    @pl.when(j == 0)
    def _():
        hbuf[0:CONV_HALO, :] = jnp.zeros((CONV_HALO, C_CH), F32)

    @pl.when(j > 0)
    def _():
        hbuf[0:CONV_HALO, :] = hbuf[tm:tm + CONV_HALO, :]

    xb = x_ref[...].astype(BF16)
    u = jnp.dot(xb, w_ref[...], preferred_element_type=F32)
    hbuf[CONV_HALO:CONV_HALO + tm, :] = u[:, :C_CH] * _sigmoid(u[:, C_CH:])
    first = CONV_HALO - (CONV_W - 1)
    for sb in range(tm // CONV_ROWS):
        r0 = sb * CONV_ROWS
        acc = jnp.zeros((CONV_ROWS, C_CH), F32) + cb_ref[...]
        for t in range(CONV_W):
            s0 = r0 + first + t
            acc = acc + hbuf[s0:s0 + CONV_ROWS, :] * cw_ref[t:t + 1, :]
        y = _layer_norm(acc, g_ref[...], b_ref[...])
        o_ref[r0:r0 + CONV_ROWS, :] = (y * _sigmoid(y)).astype(BF16)


def _conv_branch(x2, w_c, cw, cb, g, b, batch, seq, tm):
    T, D = x2.shape
    nj = seq // tm
    row = lambda w: pl.BlockSpec((tm, w), lambda bi, j: (bi * nj + j, 0))
    full = lambda a: pl.BlockSpec(a.shape, lambda bi, j: (0, 0))
    return pl.pallas_call(
        functools.partial(_conv_kernel, tm=tm),
        grid=(batch, nj),
        in_specs=[row(D), full(w_c), full(cw), full(cb), full(g), full(b)],
        out_specs=row(C_CH),
        out_shape=jax.ShapeDtypeStruct((T, C_CH), BF16),
        scratch_shapes=[pltpu.VMEM((tm + CONV_HALO, C_CH), F32)],
        compiler_params=_cparams(("arbitrary", "arbitrary")),
        name="conv_branch",
    )(x2, w_c, cw, cb, g, b)


def _dsa_kernel(qa_ref, qi_ref, wi_ref, kk_ref, vt_ref, o_ref, keys_ref, m_ref, l_ref, acc_ref,
                *, tq, kc, top_k, idx_bits):
    i = pl.program_id(1)
    nkc = ((i + 1) * tq + kc - 1) // kc
    int_min = jnp.int32(INT_MIN)
    kthf = jnp.float32(top_k)
    nt = (((1,), (1,)), ((), ()))

    qi = qi_ref[...]
    qi_st = jnp.concatenate([qi[:, h * LANES:(h + 1) * LANES] for h in range(IDX_HEADS)], axis=0)
    w = wi_ref[0]
    q_pos = i * tq + lax.broadcasted_iota(I32, (kc, tq), 1)
    k_iota = lax.broadcasted_iota(I32, (kc, tq), 0)

    def score_body(c, carry):
        k0 = pl.multiple_of(c * kc, kc)
        kblk = kk_ref[pl.ds(k0, kc), :]
        lg = lax.dot_general(kblk, qi_st, nt, preferred_element_type=F32)
        sc = w[0:1, :] * jnp.maximum(lg[:, 0:tq], 0.0)
        for h in range(1, IDX_HEADS):
            sc = sc + w[h:h + 1, :] * jnp.maximum(lg[:, h * tq:(h + 1) * tq], 0.0)
        bits = lax.bitcast_convert_type(sc, I32)
        key = jnp.where(bits < 0, int_min - bits, bits)
        keys_ref[c] = jnp.where(k0 + k_iota <= q_pos, key, int_min)
        return carry

    lax.fori_loop(0, nkc, score_body, 0)

    def count(pred):
        def body(c, acc):
            return acc + _reduce_rows(jnp.add, jnp.where(pred(c, keys_ref[c]), 1.0, 0.0))
        return lax.fori_loop(0, nkc, body, jnp.zeros((1, tq), F32))

    def count_ge(cand):
        return count(lambda c, kb: kb >= cand)

    thr0 = jnp.where(count_ge(jnp.zeros((1, tq), I32)) >= kthf, jnp.int32(0), int_min)

    def bit_body(b, thr):
        cand = thr + lax.shift_left(jnp.int32(1), jnp.int32(30) - b)
        return jnp.where(count_ge(cand) >= kthf, cand, thr)

    thr = lax.fori_loop(0, 31, bit_body, thr0)

    need = jnp.logical_and(count_ge(thr) > kthf, thr > int_min)
    any_need = jnp.max(jnp.where(need, 1.0, 0.0)) > 0.0

    @pl.when(any_need)
    def _():
        r = kthf - count_ge(thr + 1)

        def jbody(b, j0):
            cand = j0 + lax.shift_left(jnp.int32(1), jnp.int32(idx_bits - 1) - b)
            cnt = count(lambda c, kb: jnp.logical_and(kb == thr, c * kc + k_iota < cand))
            return jnp.where(cnt < r, cand, j0)

        j0 = lax.fori_loop(0, idx_bits, jbody, jnp.zeros((1, tq), I32))

        def drop_body(c, carry):
            kb = keys_ref[c]
            drop = jnp.logical_and(jnp.logical_and(need, kb == thr), c * kc + k_iota > j0)
            keys_ref[c] = jnp.where(drop, int_min, kb)
            return carry

        lax.fori_loop(0, nkc, drop_body, 0)

    thr_sel = jnp.maximum(thr, int_min + 1)

    q = qa_ref[...]
    q_st = jnp.concatenate([q[:, h * LANES:(h + 1) * LANES] for h in range(A_HEADS)], axis=0)
    m_ref[...] = jnp.full(m_ref.shape, NEG_BIG, F32)
    l_ref[...] = jnp.zeros(l_ref.shape, F32)
    acc_ref[...] = jnp.zeros(acc_ref.shape, F32)

    def att_body(c, carry):
        k0 = pl.multiple_of(c * kc, kc)
        kblk = kk_ref[pl.ds(k0, kc), :]
        s = lax.dot_general(kblk, q_st, nt, preferred_element_type=F32)
        bias = jnp.where(keys_ref[c] >= thr_sel, 0.0, NEG_BIG)
        s = s + jnp.concatenate([bias] * A_HEADS, axis=1)
        m_prev = m_ref[...]
        m_new = jnp.maximum(m_prev, jnp.max(s, axis=0, keepdims=True))
        alpha = jnp.exp(m_prev - m_new)
        p = jnp.exp(s - m_new)
        l_ref[...] = alpha * l_ref[...] + jnp.sum(p, axis=0, keepdims=True)
        acc_ref[...] = alpha * acc_ref[...] + jnp.dot(vt_ref[c], p.astype(BF16), preferred_element_type=F32)
        m_ref[...] = m_new
        return carry

    lax.fori_loop(0, nkc, att_body, 0)
    out = acc_ref[...] / l_ref[...]
    for hp in range(A_HEADS // 2):
        pair = jnp.concatenate([out[:, (2 * hp) * tq:(2 * hp + 1) * tq],
                                out[:, (2 * hp + 1) * tq:(2 * hp + 2) * tq]], axis=0)
        o_ref[:, hp * LANES:(hp + 1) * LANES] = pair.T.astype(BF16)


def _dsa_attention(qa, qi, wit, kk, vat, batch, seq, tq, kc):
    T = qa.shape[0]
    nq = seq // tq
    r = kc // tq
    top_k = min(TOPK_MAX, seq // 4)
    qrow = lambda w: pl.BlockSpec((tq, w), lambda b, i: (b * nq + i, 0))
    return pl.pallas_call(
        functools.partial(_dsa_kernel, tq=tq, kc=kc, top_k=top_k, idx_bits=(seq - 1).bit_length()),
        grid=(batch, nq),
        in_specs=[qrow(A_HEADS * LANES), qrow(IDX_HEADS * LANES),
                  pl.BlockSpec((1, 8, tq), lambda b, i: ((b * nq + i) // r, 0, (b * nq + i) % r)),
                  pl.BlockSpec((seq, LANES), lambda b, i: (b, 0)),
                  pl.BlockSpec((seq // kc, A_HEAD_DIM, kc), lambda b, i: (b, 0, 0))],
        out_specs=qrow(A_HEADS * A_HEAD_DIM),
        out_shape=jax.ShapeDtypeStruct((T, A_HEADS * A_HEAD_DIM), BF16),
        scratch_shapes=[pltpu.VMEM((seq // kc, kc, tq), I32),
                        pltpu.VMEM((1, A_HEADS * tq), F32),
                        pltpu.VMEM((1, A_HEADS * tq), F32),
                        pltpu.VMEM((A_HEAD_DIM, A_HEADS * tq), F32)],
        compiler_params=_cparams(("parallel", "arbitrary")),
        name="dsa_attention",
    )(qa, qi, wit, kk, vat)


def _mla_kernel(q_ref, k_ref, vt_ref, o_ref, m_ref, l_ref, acc_ref, *, tq):
    i = pl.program_id(2)
    nt = (((1,), (1,)), ((), ()))
    k_pos = lax.broadcasted_iota(I32, (tq, tq), 0)
    q_pos = lax.broadcasted_iota(I32, (tq, tq), 1)
    outs = []
    for hh in range(2):
        q = q_ref[:, hh * LANES:(hh + 1) * LANES]
        m_ref[...] = jnp.full(m_ref.shape, NEG_BIG, F32)
        l_ref[...] = jnp.zeros(l_ref.shape, F32)
        acc_ref[...] = jnp.zeros(acc_ref.shape, F32)

        def step(c, masked):
            k0 = pl.multiple_of(c * tq, tq)
            kblk = k_ref[pl.ds(k0, tq), hh * LANES:(hh + 1) * LANES]
            vt = vt_ref[c, hh * B_V:(hh + 1) * B_V, :]
            s = lax.dot_general(kblk, q, nt, preferred_element_type=F32)
            if masked:
                s = jnp.where(k_pos <= q_pos, s, NEG_BIG)
            m_prev = m_ref[...]
            m_new = jnp.maximum(m_prev, jnp.max(s, axis=0, keepdims=True))
            alpha = jnp.exp(m_prev - m_new)
            p = jnp.exp(s - m_new)
            l_ref[...] = alpha * l_ref[...] + jnp.sum(p, axis=0, keepdims=True)
            acc_ref[...] = alpha * acc_ref[...] + jnp.dot(vt, p.astype(BF16), preferred_element_type=F32)
            m_ref[...] = m_new

        def body(c, carry):
            step(c, False)
            return carry

        lax.fori_loop(0, i, body, 0)
        step(i, True)
        outs.append(acc_ref[...] / l_ref[...])
    o_ref[...] = jnp.concatenate(outs, axis=0).T.astype(BF16)


def _mla_attention(q, k, vt, batch, seq, tq):
    T = q.shape[0]
    nq = seq // tq
    pairs = B_HEADS // 2
    return pl.pallas_call(
        functools.partial(_mla_kernel, tq=tq),
        grid=(batch, pairs, nq),
        in_specs=[pl.BlockSpec((tq, 2 * LANES), lambda b, h, i: (b * nq + i, h)),
                  pl.BlockSpec((seq, 2 * LANES), lambda b, h, i: (b, h)),
                  pl.BlockSpec((nq, 2 * B_V, tq), lambda b, h, i: (b, h, 0))],
        out_specs=pl.BlockSpec((tq, 2 * B_V), lambda b, h, i: (b * nq + i, h)),
        out_shape=jax.ShapeDtypeStruct((T, B_HEADS * B_V), BF16),
        scratch_shapes=[pltpu.VMEM((1, tq), F32), pltpu.VMEM((1, tq), F32), pltpu.VMEM((B_V, tq), F32)],
        compiler_params=_cparams(("parallel", "parallel", "arbitrary")),
        name="mla_attention",
    )(q, k, vt)


def _merge_kernel(x_ref, ya_ref, yb_ref, yc_ref, wg_ref, bg_ref, wbr_ref, wo_ref, g_ref, b_ref, o_ref,
                  *, alpha, d):
    x = x_ref[...]
    xb = x.astype(BF16)
    merged = None
    for k, y_ref in enumerate((ya_ref, yb_ref, yc_ref)):
        pre = jnp.dot(xb, wg_ref[:, k * d:(k + 1) * d], preferred_element_type=F32) + bg_ref[k:k + 1, :]
        br = jnp.dot(y_ref[...], wbr_ref[k], preferred_element_type=F32)
        term = _sigmoid(pre) * br
        merged = term if merged is None else merged + term
    mix = jnp.dot(merged.astype(BF16), wo_ref[...], preferred_element_type=F32)
    o_ref[...] = _layer_norm(alpha * x + mix, g_ref[...], b_ref[...])


def _merge(x2, ya, yb, yc, wg, bg, wbr, wo, g, b, alpha, tm):
    T, D = x2.shape
    row = lambda w: pl.BlockSpec((tm, w), lambda i: (i, 0))
    full = lambda a: pl.BlockSpec(a.shape, lambda i: (0,) * a.ndim)
    return pl.pallas_call(
        functools.partial(_merge_kernel, alpha=alpha, d=D),
        grid=(T // tm,),
        in_specs=[row(D), row(ya.shape[1]), row(yb.shape[1]), row(yc.shape[1]),
                  full(wg), full(bg), full(wbr), full(wo), full(g), full(b)],
        out_specs=row(D),
        out_shape=jax.ShapeDtypeStruct((T, D), F32),
        compiler_params=_cparams(("parallel",)),
        name="merge_ln1",
    )(x2, ya, yb, yc, wg, bg, wbr, wo, g, b)


def _ffn_kernel(x_ref, p_ref, wg_ref, wu_ref, wo_ref, wpg_ref, wpp_ref, g_ref, b_ref, o_ref, acc_ref,
                *, alpha):
    j = pl.program_id(1)
    xb = x_ref[...].astype(BF16)

    @pl.when(j == 0)
    def _():
        gate = _sigmoid(jnp.dot(xb, wpg_ref[...], preferred_element_type=F32))
        emb = jnp.dot(p_ref[...].astype(BF16), wpp_ref[...], preferred_element_type=F32)
        acc_ref[...] = alpha * x_ref[...] + gate * emb

    fg = jnp.dot(xb, wg_ref[...], preferred_element_type=F32)
    fu = jnp.dot(xb, wu_ref[...], preferred_element_type=F32)
    a = (fg * _sigmoid(fg) * fu).astype(BF16)
    acc_ref[...] += jnp.dot(a, wo_ref[...], preferred_element_type=F32)

    @pl.when(j == pl.num_programs(1) - 1)
    def _():
        o_ref[...] = _layer_norm(acc_ref[...], g_ref[...], b_ref[...])


def _ffn(x2, p2, wg, wu, wo, wpg, wpp, g, b, alpha, tm, tf):
    T, D = x2.shape
    ff = wg.shape[1]
    row = lambda w: pl.BlockSpec((tm, w), lambda i, j: (i, 0))
    full = lambda a: pl.BlockSpec(a.shape, lambda i, j: (0, 0))
    return pl.pallas_call(
        functools.partial(_ffn_kernel, alpha=alpha),
        grid=(T // tm, ff // tf),
        in_specs=[row(D), row(p2.shape[1]),
                  pl.BlockSpec((D, tf), lambda i, j: (0, j)),
                  pl.BlockSpec((D, tf), lambda i, j: (0, j)),
                  pl.BlockSpec((tf, D), lambda i, j: (j, 0)),
                  full(wpg), full(wpp), full(g), full(b)],
        out_specs=row(D),
        out_shape=jax.ShapeDtypeStruct((T, D), F32),
        scratch_shapes=[pltpu.VMEM((tm, D), F32)],
        compiler_params=_cparams(("parallel", "arbitrary")),
        name="ffn_ple_ln2",
    )(x2, p2, wg, wu, wo, wpg, wpp, g, b)


def _rope_tables(positions, dim, lane_lo, period):
    half = dim // 2
    inv = 1.0 / (ROPE_THETA ** (jnp.arange(0, dim, 2, dtype=F32) / dim))
    ang = positions.reshape(-1).astype(F32)[:, None] * inv
    cos, sin = jnp.cos(ang), jnp.sin(ang)
    lane = jnp.arange(LANES)
    rel = lane % period - lane_lo
    inside = (rel >= 0) & (rel < dim)
    f = jnp.where(inside, rel % half, 0)
    lo = inside & (rel < half)
    hi = inside & (rel >= half)
    c = jnp.where(inside[None], cos[:, f], 1.0)
    sa = jnp.where(lo[None], -sin[:, f], 0.0)
    sb = jnp.where(hi[None], sin[:, f], 0.0)
    return c, sa, sb


def _pad_cols(w, n):
    return jnp.pad(w, ((0, 0), (0, n - w.shape[1])))


def _prep_layer(w_in, w_uq, w_ukv, w_ffn_in):
    d = w_in.shape[0]
    sizes = (A_HEADS * A_HEAD_DIM, A_HEAD_DIM, A_HEAD_DIM, IDX_HEADS * IDX_DIM, IDX_DIM, IDX_HEADS,
             Q_LORA, KV_LORA, B_ROPE, 2 * C_CH, N_BRANCH * d)
    splits = [sum(sizes[:k + 1]) for k in range(len(sizes) - 1)]
    qa, ka, va, qi, ki, wi, cq, ckv, kr, conv, gate = jnp.split(w_in, splits, axis=1)
    qa = jnp.pad((qa * A_HEAD_DIM ** -0.5).reshape(d, A_HEADS, A_HEAD_DIM),
                 ((0, 0), (0, 0), (0, LANES - A_HEAD_DIM))).reshape(d, A_HEADS * LANES)
    qi = jnp.pad(qi.reshape(d, IDX_HEADS, IDX_DIM),
                 ((0, 0), (0, 0), (LANES - IDX_DIM, 0))).reshape(d, IDX_HEADS * LANES)
    wi = wi * (IDX_DIM ** -0.5 * IDX_HEADS ** -0.5)
    w_a = jnp.concatenate([qa, qi, ka, ki, _pad_cols(va, LANES), _pad_cols(wi, LANES)], axis=1)
    kr = jnp.pad(kr, ((0, 0), (B_NOPE, LANES - B_NOPE - B_ROPE)))
    w_b = jnp.concatenate([cq, ckv, kr], axis=1)
    wuq = jnp.pad(w_uq.reshape(-1, B_HEADS, B_NOPE + B_ROPE),
                  ((0, 0), (0, 0), (0, LANES - B_NOPE - B_ROPE))).reshape(-1, B_HEADS * LANES)
    ukv = w_ukv.reshape(-1, B_HEADS, B_NOPE + B_V)
    wuk = jnp.pad(ukv[:, :, :B_NOPE], ((0, 0), (0, 0), (0, LANES - B_NOPE))).reshape(-1, B_HEADS * LANES)
    wuv = ukv[:, :, B_NOPE:].reshape(-1, B_HEADS * B_V)
    wukv = jnp.concatenate([wuk, wuv], axis=1)
    ff = w_ffn_in.shape[1] // 2
    bf = lambda a: a.astype(BF16)
    return dict(w_a=bf(w_a), w_b=bf(w_b), w_c=bf(conv), w_g=bf(gate), wuq=bf(wuq), wukv=bf(wukv),
                wfg=bf(w_ffn_in[:, :ff]), wfu=bf(w_ffn_in[:, ff:]))


def _ffn_tile(ff):
    for tf in (512, 256, 128):
        if ff % tf == 0:
            return tf
    return ff


def kernel(x, p, positions, w_in, b_gate, q_norm_g, w_uq, kv_norm_g, w_ukv, conv_w, conv_b, conv_ln_g,
           conv_ln_b, w_branch, w_out, ln1_g, ln1_b, w_ffn_in, w_ffn_out, w_ple_gate, w_ple_proj, ln2_g, ln2_b):
    batch, seq, d = x.shape
    depth = w_in.shape[0]
    T = batch * seq
    alpha = float((2 * depth) ** 0.25)
    tm = min(512, seq)
    tq_a = min(128, seq)
    tabs_a = _rope_tables(positions, A_HEAD_DIM, 0, A_HEAD_DIM)
    tabs_b = _rope_tables(positions, B_ROPE, B_NOPE, LANES)
    row = lambda v: v.reshape(1, -1)
    x2 = x.reshape(T, d)
    for i in range(depth):
        w = _prep_layer(w_in[i], w_uq[i], w_ukv[i], w_ffn_in[i])
        qa, qi, kk, vat, wit = _proj_a(x2, w["w_a"], tabs_a, tm)
        qb, kb, vbt = _proj_b(x2, w["w_b"], row(q_norm_g[i]), row(kv_norm_g[i]), w["wuq"], w["wukv"], tabs_b, tm)
        y_c = _conv_branch(x2, w["w_c"], conv_w[i], row(conv_b[i]), row(conv_ln_g[i]), row(conv_ln_b[i]),
                           batch, seq, tm)
        y_a = _dsa_attention(qa, qi, wit, kk, vat, batch, seq, tq_a, tm)
        y_b = _mla_attention(qb, kb, vbt, batch, seq, tm)
        x2 = _merge(x2, y_a, y_b, y_c, w["w_g"], b_gate[i], w_branch[i].astype(BF16), w_out[i].astype(BF16),
                    row(ln1_g[i]), row(ln1_b[i]), alpha, min(256, seq))
        x2 = _ffn(x2, p[i].reshape(T, -1), w["wfg"], w["wfu"], w_ffn_out[i].astype(BF16),
                  w_ple_gate[i].astype(BF16), w_ple_proj[i].astype(BF16), row(ln2_g[i]), row(ln2_b[i]),
                  alpha, tm, _ffn_tile(w_ffn_in.shape[2] // 2))
    return x2.reshape(batch, seq, d)
```

```python
import functools

import jax
import jax.numpy as jnp
from jax import lax
from jax.experimental import pallas as pl
from jax.experimental.pallas import tpu as pltpu

F32 = jnp.float32
BF16 = jnp.bfloat16
I32 = jnp.int32

LANES = 128
SUBLANES = 8

PLE_DIM = 256
A_HEADS = 8
A_HEAD_DIM = 64
IDX_HEADS = 4
IDX_DIM = 64
TOPK_MAX = 256
B_HEADS = 8
B_NOPE = 64
B_ROPE = 32
B_V = 64
Q_LORA = 384
KV_LORA = 256
C_CH = 512
CONV_W = 31
N_BRANCH = 3
ROPE_THETA = 10000.0
LN_EPS = 1e-5
RMS_EPS = 1e-6

INT_MIN = -(2 ** 31)
NEG_BIG = -1e30
LOG2E = 1.4426950408889634

VMEM_LIMIT = 56 * 1024 * 1024


def _cparams(sem):
    return pltpu.CompilerParams(dimension_semantics=sem, vmem_limit_bytes=VMEM_LIMIT)


def _layer_norm(v, g, b):
    mu = jnp.mean(v, axis=-1, keepdims=True)
    d = v - mu
    var = jnp.mean(d * d, axis=-1, keepdims=True)
    return d * lax.rsqrt(var + LN_EPS) * g + b


def _sigmoid(v):
    return 1.0 / (1.0 + jnp.exp(-v))


def _tree(op, parts):
    while len(parts) > 1:
        parts = [op(a, b) for a, b in zip(parts[::2], parts[1::2])] + ([parts[-1]] if len(parts) % 2 else [])
    return parts[0]


def _reduce_rows(op, v):
    slab = _tree(op, [v[r:r + SUBLANES] for r in range(0, v.shape[0], SUBLANES)])
    return op.reduce(slab, axis=0, keepdims=True)


def _rope_group(h, c, sa, sb, half):
    return h * c + pltpu.roll(h, LANES - half, 1) * sa + pltpu.roll(h, half, 1) * sb


def _proj_a_kernel(x_ref, w_ref, c_ref, sa_ref, sb_ref, qa_ref, qi_ref, kk_ref, va_ref, wi_ref):
    xb = x_ref[...].astype(BF16)
    h = jnp.dot(xb, w_ref[...], preferred_element_type=F32)
    c, sa, sb = c_ref[...], sa_ref[...], sb_ref[...]
    half = A_HEAD_DIM // 2
    for g in range(A_HEADS):
        hg = h[:, g * LANES:(g + 1) * LANES]
        qa_ref[:, g * LANES:(g + 1) * LANES] = _rope_group(hg, c, sa, sb, half).astype(BF16)
    off = A_HEADS
    for g in range(IDX_HEADS):
        hg = h[:, (off + g) * LANES:(off + g + 1) * LANES]
        qi_ref[:, g * LANES:(g + 1) * LANES] = _rope_group(hg, c, sa, sb, half).astype(BF16)
    off += IDX_HEADS
    hg = h[:, off * LANES:(off + 1) * LANES]
    kk_ref[...] = _rope_group(hg, c, sa, sb, half).astype(BF16)
    off += 1
    va_ref[0] = h[:, off * LANES:(off + 1) * LANES].T[:A_HEAD_DIM, :].astype(BF16)
    off += 1
    wi_ref[0] = h[:, off * LANES:(off + 1) * LANES].T[:8, :]


def _proj_a(x2, w_a, tabs, tm):
    T, D = x2.shape
    n = w_a.shape[1]
    row = lambda w: pl.BlockSpec((tm, w), lambda i: (i, 0))
    return pl.pallas_call(
        _proj_a_kernel,
        grid=(T // tm,),
        in_specs=[row(D), pl.BlockSpec((D, n), lambda i: (0, 0)), row(LANES), row(LANES), row(LANES)],
        out_specs=[row(A_HEADS * LANES), row(IDX_HEADS * LANES), row(LANES),
                   pl.BlockSpec((1, A_HEAD_DIM, tm), lambda i: (i, 0, 0)),
                   pl.BlockSpec((1, 8, tm), lambda i: (i, 0, 0))],
        out_shape=[jax.ShapeDtypeStruct((T, A_HEADS * LANES), BF16),
                   jax.ShapeDtypeStruct((T, IDX_HEADS * LANES), BF16),
                   jax.ShapeDtypeStruct((T, LANES), BF16),
                   jax.ShapeDtypeStruct((T // tm, A_HEAD_DIM, tm), BF16),
                   jax.ShapeDtypeStruct((T // tm, 8, tm), F32)],
        compiler_params=_cparams(("parallel",)),
        name="proj_a",
    )(x2, w_a, *tabs)


BIAS_LANE = B_NOPE + B_ROPE


def _proj_b_kernel(x_ref, w_ref, gq_ref, gkv_ref, wuq_ref, wukv_ref, c_ref, sa_ref, sb_ref,
                   q_ref, k_ref, v_ref, *, q_scale):
    xb = x_ref[...].astype(BF16)
    h = jnp.dot(xb, w_ref[...], preferred_element_type=F32)
    c, sa, sb = c_ref[...], sa_ref[...], sb_ref[...]
    half = B_ROPE // 2
    cq = h[:, :Q_LORA]
    cqn = cq * lax.rsqrt(jnp.mean(cq * cq, axis=-1, keepdims=True) + RMS_EPS) * gq_ref[...]
    ckv = h[:, Q_LORA:Q_LORA + KV_LORA]
    ckvn = ckv * lax.rsqrt(jnp.mean(ckv * ckv, axis=-1, keepdims=True) + RMS_EPS) * gkv_ref[...]
    kr = _rope_group(h[:, Q_LORA + KV_LORA:], c, sa, sb, half)
    q = jnp.dot(cqn.astype(BF16), wuq_ref[...], preferred_element_type=F32)
    kv = jnp.dot(ckvn.astype(BF16), wukv_ref[...], preferred_element_type=F32)
    one_hot = jnp.where(lax.broadcasted_iota(I32, (1, LANES), 1) == BIAS_LANE, 1.0, 0.0)
    for g in range(B_HEADS):
        qg = q[:, g * LANES:(g + 1) * LANES]
        q_ref[:, g * LANES:(g + 1) * LANES] = (_rope_group(qg, c, sa, sb, half) * q_scale + one_hot).astype(BF16)
        k_ref[:, g * LANES:(g + 1) * LANES] = (kv[:, g * LANES:(g + 1) * LANES] + kr).astype(BF16)
    v_ref[0] = kv[:, B_HEADS * LANES:].T.astype(BF16)


def _proj_b(x2, w_b, gq, gkv, wuq, wukv, tabs, tm):
    T, D = x2.shape
    row = lambda w: pl.BlockSpec((tm, w), lambda i: (i, 0))
    full = lambda a: pl.BlockSpec(a.shape, lambda i: (0, 0))
    q_scale = float((B_NOPE + B_ROPE) ** -0.5 * LOG2E)
    return pl.pallas_call(
        functools.partial(_proj_b_kernel, q_scale=q_scale),
        grid=(T // tm,),
        in_specs=[row(D), full(w_b), full(gq), full(gkv), full(wuq), full(wukv),
                  row(LANES), row(LANES), row(LANES)],
        out_specs=[row(B_HEADS * LANES), row(B_HEADS * LANES),
                   pl.BlockSpec((1, B_HEADS * B_V, tm), lambda i: (i, 0, 0))],
        out_shape=[jax.ShapeDtypeStruct((T, B_HEADS * LANES), BF16),
                   jax.ShapeDtypeStruct((T, B_HEADS * LANES), BF16),
                   jax.ShapeDtypeStruct((T // tm, B_HEADS * B_V, tm), BF16)],
        compiler_params=_cparams(("parallel",)),
        name="proj_b",
    )(x2, w_b, gq, gkv, wuq, wukv, *tabs)


CONV_HALO = 32
CONV_ROWS = 64


def _conv_kernel(x_ref, w_ref, cw_ref, cb_ref, g_ref, b_ref, o_ref, hbuf, *, tm):
    j = pl.program_id(1)
    span = CONV_HALO + tm

    @pl.when(j == 0)
    def _():
        hbuf[0, 0:CONV_HALO, :] = jnp.zeros((CONV_HALO, C_CH), F32)

    @pl.when(j > 0)
    def _():
        hbuf[0, 0:CONV_HALO, :] = hbuf[0, tm:span, :]

    xb = x_ref[...].astype(BF16)
    u = jnp.dot(xb, w_ref[...], preferred_element_type=F32)
    hbuf[0, CONV_HALO:span, :] = u[:, :C_CH] * _sigmoid(u[:, C_CH:])
    for r in range(1, SUBLANES):
        hbuf[r, SUBLANES:span, :] = hbuf[0, SUBLANES - r:span - r, :]
    for sb in range(tm // CONV_ROWS):
        r0 = sb * CONV_ROWS
        acc = jnp.zeros((CONV_ROWS, C_CH), F32) + cb_ref[...]
        for t in range(CONV_W):
            back = CONV_W - 1 - t
            r = back % SUBLANES
            s0 = CONV_HALO + r0 - back + r
            acc = acc + hbuf[r, s0:s0 + CONV_ROWS, :] * cw_ref[t:t + 1, :]
        y = _layer_norm(acc, g_ref[...], b_ref[...])
        o_ref[r0:r0 + CONV_ROWS, :] = (y * _sigmoid(y)).astype(BF16)


def _conv_branch(x2, w_c, cw, cb, g, b, batch, seq, tm):
    T, D = x2.shape
    nj = seq // tm
    row = lambda w: pl.BlockSpec((tm, w), lambda bi, j: (bi * nj + j, 0))
    full = lambda a: pl.BlockSpec(a.shape, lambda bi, j: (0, 0))
    return pl.pallas_call(
        functools.partial(_conv_kernel, tm=tm),
        grid=(batch, nj),
        in_specs=[row(D), full(w_c), full(cw), full(cb), full(g), full(b)],
        out_specs=row(C_CH),
        out_shape=jax.ShapeDtypeStruct((T, C_CH), BF16),
        scratch_shapes=[pltpu.VMEM((SUBLANES, tm + CONV_HALO, C_CH), F32)],
        compiler_params=_cparams(("arbitrary", "arbitrary")),
        name="conv_branch",
    )(x2, w_c, cw, cb, g, b)


WORD = 32
ONES_ROWS = 16


def _bit_planes(words):
    a = list(words)
    j, m = 16, 0x0000FFFF
    while j:
        k = 0
        while k < WORD:
            t = (a[k] ^ lax.shift_right_logical(a[k + j], jnp.int32(j))) & jnp.int32(m)
            a[k] = a[k] ^ t
            a[k + j] = a[k + j] ^ lax.shift_left(t, jnp.int32(j))
            k = (k + j + 1) & ~j
        j >>= 1
        m = (m ^ (m << j)) & 0x7FFFFFFF
    return a


def _dsa_kernel(qa_ref, qi_ref, wi_ref, kk_ref, vt_ref, o_ref, keys_ref, planes_ref, m_ref, acc_ref, sa_ref, sb_ref,
                *, tq, kc, top_k, idx_bits):
    i = pl.program_id(1)
    nkc = ((i + 1) * tq + kc - 1) // kc
    int_min = jnp.int32(INT_MIN)
    kthf = jnp.float32(top_k)
    nt = (((1,), (1,)), ((), ()))
    wrows = kc // WORD
    nwords = planes_ref.shape[1]
    nchunks = keys_ref.shape[0]

    qi = qi_ref[...]
    qi_st = jnp.concatenate([qi[:, h * LANES:(h + 1) * LANES] for h in range(IDX_HEADS)], axis=0)
    w = wi_ref[0]
    q_pos = i * tq + lax.broadcasted_iota(I32, (kc, tq), 1)
    k_iota = lax.broadcasted_iota(I32, (kc, tq), 0)

    @pl.when(i == 0)
    def _():
        planes_ref[...] = jnp.zeros(planes_ref.shape, I32)

    def score_chunk(c, diagonal):
        k0 = pl.multiple_of(c * kc, kc)
        kblk = kk_ref[pl.ds(k0, kc), :]
        lg = lax.dot_general(kblk, qi_st, nt, preferred_element_type=F32)
        sc = w[0:1, :] * jnp.maximum(lg[:, 0:tq], 0.0)
        for h in range(1, IDX_HEADS):
            sc = sc + w[h:h + 1, :] * jnp.maximum(lg[:, h * tq:(h + 1) * tq], 0.0)
        bits = lax.bitcast_convert_type(sc, I32)
        key = jnp.where(bits < 0, int_min - bits, bits)
        if diagonal:
            key = jnp.where(k0 + k_iota <= q_pos, key, int_min)
        keys_ref[c] = key
        w0 = pl.multiple_of(c * wrows, wrows)
        for g in range(wrows // SUBLANES):
            base = g * WORD * SUBLANES
            planes = _bit_planes([key[base + SUBLANES * j:base + SUBLANES * (j + 1)] for j in range(WORD)])
            planes[0] = ~planes[0]
            for s in range(WORD):
                planes_ref[s, pl.ds(w0 + SUBLANES * g, SUBLANES), :] = planes[s]

    def score_body(c, carry):
        score_chunk(c, False)
        return carry

    lax.fori_loop(0, nkc - 1, score_body, 0)
    score_chunk(nkc - 1, True)

    word_row = lax.broadcasted_iota(I32, (nwords, tq), 0)
    act0 = jnp.where(word_row < nkc * wrows, jnp.int32(-1), jnp.int32(0))

    def popcount_rows(v):
        cnt = _tree(jnp.add, [lax.population_count(v[r:r + SUBLANES]) for r in range(0, nwords, SUBLANES)])
        return jnp.sum(cnt.astype(F32), axis=0, keepdims=True)

    def bit_body(s, carry):
        act, rem, thr_u = carry
        cand = act & planes_ref[s]
        c1 = popcount_rows(cand)
        take = c1 >= rem
        act = jnp.where(take, cand, act ^ cand)
        rem = jnp.where(take, rem, rem - c1)
        thr_u = thr_u | jnp.where(take, lax.shift_left(jnp.int32(1), jnp.int32(31) - s), jnp.int32(0))
        return act, rem, thr_u

    act, rem, thr_u = lax.fori_loop(0, WORD, bit_body,
                                    (act0, jnp.full((1, tq), kthf, F32), jnp.zeros((1, tq), I32)))
    thr = thr_u ^ int_min

    need = jnp.logical_and(popcount_rows(act) > rem, thr > int_min)
    any_need = jnp.max(jnp.where(need, 1.0, 0.0)) > 0.0

    @pl.when(any_need)
    def _():
        def count_tied_below(cand):
            def body(c, acc):
                hit = jnp.logical_and(keys_ref[c] == thr, c * kc + k_iota < cand)
                return acc + _reduce_rows(jnp.add, jnp.where(hit, 1.0, 0.0))
            return lax.fori_loop(0, nkc, body, jnp.zeros((1, tq), F32))

        def jbody(b, j0):
            cand = j0 + lax.shift_left(jnp.int32(1), jnp.int32(idx_bits - 1) - b)
            return jnp.where(count_tied_below(cand) < rem, cand, j0)

        j0 = lax.fori_loop(0, idx_bits, jbody, jnp.zeros((1, tq), I32))

        def drop_body(c, carry):
            kb = keys_ref[c]
            drop = jnp.logical_and(jnp.logical_and(need, kb == thr), c * kc + k_iota > j0)
            keys_ref[c] = jnp.where(drop, int_min, kb)
            return carry

        lax.fori_loop(0, nkc, drop_body, 0)

    thr_sel = jnp.maximum(thr, int_min + 1)

    q = qa_ref[...]
    eye = jnp.where(lax.broadcasted_iota(I32, (tq, tq), 0) == lax.broadcasted_iota(I32, (tq, tq), 1),
                    1.0, 0.0).astype(BF16)
    npairs = A_HEADS // 2
    q_pairs = [jnp.concatenate([jnp.concatenate([q[:, h * LANES:(h + 1) * LANES], eye], axis=1)
                                for h in (2 * hp, 2 * hp + 1)], axis=0) for hp in range(npairs)]
    ones = jnp.ones((ONES_ROWS, kc), BF16)
    m_ref[...] = jnp.full(m_ref.shape, NEG_BIG, F32)
    acc_ref[...] = jnp.zeros(acc_ref.shape, F32)

    def keys_aug(c):
        cc = jnp.minimum(c, nkc - 1)
        k0 = pl.multiple_of(cc * kc, kc)
        bias = jnp.where(keys_ref[cc] >= thr_sel, 0.0, NEG_BIG).astype(BF16)
        bias = jnp.where(c < nkc, bias, NEG_BIG)
        return jnp.concatenate([kk_ref[pl.ds(k0, kc), :], bias], axis=1)

    def scores(k_aug, hp, s_ref):
        cols = slice(hp * 2 * tq, (hp + 1) * 2 * tq)
        s_ref[:, cols] = lax.dot_general(k_aug, q_pairs[hp], nt, preferred_element_type=F32)

    def softmax_pv(vt_aug, hp, s_ref):
        cols = slice(hp * 2 * tq, (hp + 1) * 2 * tq)
        s = s_ref[:, cols]
        m_prev = m_ref[:, cols]
        m_new = jnp.maximum(m_prev, jnp.max(s, axis=0, keepdims=True))
        alpha = jnp.exp2(m_prev - m_new)
        p = jnp.exp2(s - m_new).astype(BF16)
        acc_ref[:, cols] = alpha * acc_ref[:, cols] + jnp.dot(vt_aug, p, preferred_element_type=F32)
        m_ref[:, cols] = m_new

    def half_trip(c, cur_ref, nxt_ref):
        k_aug = keys_aug(c + 1)
        vt_aug = jnp.concatenate([vt_ref[jnp.minimum(c, nchunks - 1)], ones], axis=0)
        for hp in range(npairs):
            scores(k_aug, hp, nxt_ref)
            softmax_pv(vt_aug, hp, cur_ref)

    def att_body(j, carry):
        half_trip(2 * j, sa_ref, sb_ref)
        half_trip(2 * j + 1, sb_ref, sa_ref)
        return carry

    k_aug0 = keys_aug(0)
    for hp in range(npairs):
        scores(k_aug0, hp, sa_ref)
    lax.fori_loop(0, (nkc + 1) // 2, att_body, 0)
    out = acc_ref[0:A_HEAD_DIM, :] / acc_ref[A_HEAD_DIM:A_HEAD_DIM + 1, :]
    for hp in range(npairs):
        pair = jnp.concatenate([out[:, (2 * hp) * tq:(2 * hp + 1) * tq],
                                out[:, (2 * hp + 1) * tq:(2 * hp + 2) * tq]], axis=0)
        o_ref[:, hp * LANES:(hp + 1) * LANES] = pair.T.astype(BF16)


def _dsa_attention(qa, qi, wit, kk, vat, batch, seq, tq, kc):
    T = qa.shape[0]
    nq = seq // tq
    r = kc // tq
    top_k = min(TOPK_MAX, seq // 4)
    qrow = lambda w: pl.BlockSpec((tq, w), lambda b, i: (b * nq + i, 0))
    return pl.pallas_call(
        functools.partial(_dsa_kernel, tq=tq, kc=kc, top_k=top_k, idx_bits=(seq - 1).bit_length()),
        grid=(batch, nq),
        in_specs=[qrow(A_HEADS * LANES), qrow(IDX_HEADS * LANES),
                  pl.BlockSpec((1, 8, tq), lambda b, i: ((b * nq + i) // r, 0, (b * nq + i) % r)),
                  pl.BlockSpec((seq, LANES), lambda b, i: (b, 0)),
                  pl.BlockSpec((seq // kc, A_HEAD_DIM, kc), lambda b, i: (b, 0, 0))],
        out_specs=qrow(A_HEADS * A_HEAD_DIM),
        out_shape=jax.ShapeDtypeStruct((T, A_HEADS * A_HEAD_DIM), BF16),
        scratch_shapes=[pltpu.VMEM((seq // kc, kc, tq), I32),
                        pltpu.VMEM((WORD, seq // WORD, tq), I32),
                        pltpu.VMEM((1, A_HEADS * tq), F32),
                        pltpu.VMEM((A_HEAD_DIM + ONES_ROWS, A_HEADS * tq), F32),
                        pltpu.VMEM((kc, A_HEADS * tq), F32),
                        pltpu.VMEM((kc, A_HEADS * tq), F32)],
        compiler_params=_cparams(("arbitrary", "arbitrary")),
        name="dsa_attention",
    )(qa, qi, wit, kk, vat)


def _mla_kernel(q_ref, k_ref, vt_ref, o_ref, m_ref, acc_ref, sa_ref, sb_ref, *, tq):
    i = pl.program_id(2)
    nq = vt_ref.shape[0]
    nt = (((1,), (1,)), ((), ()))
    k_pos = lax.broadcasted_iota(I32, (tq, tq), 0)
    q_pos = lax.broadcasted_iota(I32, (tq, tq), 1)
    bias_lane = lax.broadcasted_iota(I32, (1, LANES), 1) == BIAS_LANE
    ones = jnp.ones((ONES_ROWS, tq), BF16)
    m_ref[...] = jnp.full(m_ref.shape, NEG_BIG, F32)
    acc_ref[...] = jnp.zeros(acc_ref.shape, F32)

    def scores(c, hh, s_ref, valid):
        k0 = pl.multiple_of(jnp.minimum(c, nq - 1) * tq, tq)
        kblk = k_ref[pl.ds(k0, tq), hh * LANES:(hh + 1) * LANES]
        if valid is not None:
            off = jnp.where(jnp.logical_and(bias_lane, jnp.logical_not(valid)), NEG_BIG, 0.0)
            kblk = kblk + off.astype(BF16)
        s_ref[hh] = lax.dot_general(kblk, q_ref[:, hh * LANES:(hh + 1) * LANES], nt,
                                    preferred_element_type=F32)

    def softmax_pv(c, hh, s_ref, diagonal):
        vt_aug = jnp.concatenate([vt_ref[jnp.minimum(c, nq - 1), hh * B_V:(hh + 1) * B_V, :], ones], axis=0)
        s = s_ref[hh]
        if diagonal:
            s = jnp.where(k_pos <= q_pos, s, NEG_BIG)
        m_prev = m_ref[hh]
        m_new = jnp.maximum(m_prev, jnp.max(s, axis=0, keepdims=True))
        alpha = jnp.exp2(m_prev - m_new)
        p = jnp.exp2(s - m_new).astype(BF16)
        acc_ref[hh] = alpha * acc_ref[hh] + jnp.dot(vt_aug, p, preferred_element_type=F32)
        m_ref[hh] = m_new

    def half_trip(cur, cur_ref, nxt, nxt_ref, diagonal=False):
        for hh in range(2):
            scores(nxt, hh, nxt_ref, nxt < i)
            softmax_pv(cur, hh, cur_ref, diagonal)

    for hh in range(2):
        scores(i, hh, sa_ref, None)
    half_trip(i, sa_ref, 0, sb_ref, diagonal=True)

    def trip(t, carry):
        half_trip(2 * t, sb_ref, 2 * t + 1, sa_ref)
        half_trip(2 * t + 1, sa_ref, 2 * t + 2, sb_ref)
        return carry

    lax.fori_loop(0, (i + 1) // 2, trip, 0)
    outs = [acc_ref[hh, 0:B_V, :] / acc_ref[hh, B_V:B_V + 1, :] for hh in range(2)]
    o_ref[...] = jnp.concatenate(outs, axis=0).T.astype(BF16)


def _mla_attention(q, k, vt, batch, seq, tq):
    T = q.shape[0]
    nq = seq // tq
    pairs = B_HEADS // 2
    return pl.pallas_call(
        functools.partial(_mla_kernel, tq=tq),
        grid=(batch, pairs, nq),
        in_specs=[pl.BlockSpec((tq, 2 * LANES), lambda b, h, i: (b * nq + i, h)),
                  pl.BlockSpec((seq, 2 * LANES), lambda b, h, i: (b, h)),
                  pl.BlockSpec((nq, 2 * B_V, tq), lambda b, h, i: (b, h, 0))],
        out_specs=pl.BlockSpec((tq, 2 * B_V), lambda b, h, i: (b * nq + i, h)),
        out_shape=jax.ShapeDtypeStruct((T, B_HEADS * B_V), BF16),
        scratch_shapes=[pltpu.VMEM((2, 1, tq), F32), pltpu.VMEM((2, B_V + ONES_ROWS, tq), F32),
                        pltpu.VMEM((2, tq, tq), F32), pltpu.VMEM((2, tq, tq), F32)],
        compiler_params=_cparams(("parallel", "parallel", "arbitrary")),
        name="mla_attention",
    )(q, k, vt)


def _merge_kernel(x_ref, ya_ref, yb_ref, yc_ref, wg_ref, bg_ref, wbr_ref, wo_ref, g_ref, b_ref, o_ref,
                  *, alpha, d):
    x = x_ref[...]
    xb = x.astype(BF16)
    merged = None
    for k, y_ref in enumerate((ya_ref, yb_ref, yc_ref)):
        pre = jnp.dot(xb, wg_ref[:, k * d:(k + 1) * d], preferred_element_type=F32) + bg_ref[k:k + 1, :]
        br = jnp.dot(y_ref[...], wbr_ref[k], preferred_element_type=F32)
        term = _sigmoid(pre) * br
        merged = term if merged is None else merged + term
    mix = jnp.dot(merged.astype(BF16), wo_ref[...], preferred_element_type=F32)
    o_ref[...] = _layer_norm(alpha * x + mix, g_ref[...], b_ref[...])


def _merge(x2, ya, yb, yc, wg, bg, wbr, wo, g, b, alpha, tm):
    T, D = x2.shape
    row = lambda w: pl.BlockSpec((tm, w), lambda i: (i, 0))
    full = lambda a: pl.BlockSpec(a.shape, lambda i: (0,) * a.ndim, pipeline_mode=pl.Buffered(1))
    return pl.pallas_call(
        functools.partial(_merge_kernel, alpha=alpha, d=D),
        grid=(T // tm,),
        in_specs=[row(D), row(ya.shape[1]), row(yb.shape[1]), row(yc.shape[1]),
                  full(wg), full(bg), full(wbr), full(wo), full(g), full(b)],
        out_specs=row(D),
        out_shape=jax.ShapeDtypeStruct((T, D), F32),
        compiler_params=_cparams(("parallel",)),
        name="merge_ln1",
    )(x2, ya, yb, yc, wg, bg, wbr, wo, g, b)


def _ffn_kernel(x_ref, p_ref, wg_ref, wu_ref, wo_ref, wpg_ref, wpp_ref, g_ref, b_ref, o_ref, *, alpha):
    x = x_ref[...]
    xb = x.astype(BF16)
    nchunk = wg_ref.shape[0]

    def hidden(c):
        fg = jnp.dot(xb, wg_ref[c], preferred_element_type=F32)
        fu = jnp.dot(xb, wu_ref[c], preferred_element_type=F32)
        return (fg * _sigmoid(fg) * fu).astype(BF16)

    gate = _sigmoid(jnp.dot(xb, wpg_ref[...], preferred_element_type=F32))
    emb = jnp.dot(p_ref[...].astype(BF16), wpp_ref[...], preferred_element_type=F32)
    acc = alpha * x + gate * emb
    a_prev = hidden(0)
    for c in range(1, nchunk):
        a_cur = hidden(c)
        acc = acc + jnp.dot(a_prev, wo_ref[c - 1], preferred_element_type=F32)
        a_prev = a_cur
    acc = acc + jnp.dot(a_prev, wo_ref[nchunk - 1], preferred_element_type=F32)
    o_ref[...] = _layer_norm(acc, g_ref[...], b_ref[...])


def _ffn(x2, p2, wg, wu, wo, wpg, wpp, g, b, alpha, tm):
    T, D = x2.shape
    row = lambda w: pl.BlockSpec((tm, w), lambda i: (i, 0))
    full = lambda a: pl.BlockSpec(a.shape, lambda i: (0,) * a.ndim, pipeline_mode=pl.Buffered(1))
    return pl.pallas_call(
        functools.partial(_ffn_kernel, alpha=alpha),
        grid=(T // tm,),
        in_specs=[row(D), row(p2.shape[1]), full(wg), full(wu), full(wo), full(wpg), full(wpp), full(g), full(b)],
        out_specs=row(D),
        out_shape=jax.ShapeDtypeStruct((T, D), F32),
        compiler_params=_cparams(("parallel",)),
        name="ffn_ple_ln2",
    )(x2, p2, wg, wu, wo, wpg, wpp, g, b)


def _rope_tables(positions, dim, lane_lo, period):
    half = dim // 2
    inv = 1.0 / (ROPE_THETA ** (jnp.arange(0, dim, 2, dtype=F32) / dim))
    ang = positions.reshape(-1).astype(F32)[:, None] * inv
    cos, sin = jnp.cos(ang), jnp.sin(ang)
    lane = jnp.arange(LANES)
    rel = lane % period - lane_lo
    inside = (rel >= 0) & (rel < dim)
    f = jnp.where(inside, rel % half, 0)
    lo = inside & (rel < half)
    hi = inside & (rel >= half)
    c = jnp.where(inside[None], cos[:, f], 1.0)
    sa = jnp.where(lo[None], -sin[:, f], 0.0)
    sb = jnp.where(hi[None], sin[:, f], 0.0)
    return c, sa, sb


def _pad_cols(w, n):
    return jnp.pad(w, ((0, 0), (0, n - w.shape[1])))


def _ffn_tile(ff):
    for tf in (512, 256, 128):
        if ff % tf == 0:
            return tf
    return ff


def _prep_layer(w_in, w_uq, w_ukv, w_ffn_in):
    d = w_in.shape[0]
    sizes = (A_HEADS * A_HEAD_DIM, A_HEAD_DIM, A_HEAD_DIM, IDX_HEADS * IDX_DIM, IDX_DIM, IDX_HEADS,
             Q_LORA, KV_LORA, B_ROPE, 2 * C_CH, N_BRANCH * d)
    splits = [sum(sizes[:k + 1]) for k in range(len(sizes) - 1)]
    qa, ka, va, qi, ki, wi, cq, ckv, kr, conv, gate = jnp.split(w_in, splits, axis=1)
    qa = jnp.pad((qa * (A_HEAD_DIM ** -0.5 * LOG2E)).reshape(d, A_HEADS, A_HEAD_DIM),
                 ((0, 0), (0, 0), (0, LANES - A_HEAD_DIM))).reshape(d, A_HEADS * LANES)
    qi = jnp.pad(qi.reshape(d, IDX_HEADS, IDX_DIM),
                 ((0, 0), (0, 0), (LANES - IDX_DIM, 0))).reshape(d, IDX_HEADS * LANES)
    wi = wi * (IDX_DIM ** -0.5 * IDX_HEADS ** -0.5)
    w_a = jnp.concatenate([qa, qi, ka, ki, _pad_cols(va, LANES), _pad_cols(wi, LANES)], axis=1)
    kr = jnp.pad(kr, ((0, 0), (B_NOPE, LANES - B_NOPE - B_ROPE)))
    w_b = jnp.concatenate([cq, ckv, kr], axis=1)
    wuq = jnp.pad(w_uq.reshape(-1, B_HEADS, B_NOPE + B_ROPE),
                  ((0, 0), (0, 0), (0, LANES - B_NOPE - B_ROPE))).reshape(-1, B_HEADS * LANES)
    ukv = w_ukv.reshape(-1, B_HEADS, B_NOPE + B_V)
    wuk = jnp.pad(ukv[:, :, :B_NOPE], ((0, 0), (0, 0), (0, LANES - B_NOPE))).reshape(-1, B_HEADS * LANES)
    wuv = ukv[:, :, B_NOPE:].reshape(-1, B_HEADS * B_V)
    wukv = jnp.concatenate([wuk, wuv], axis=1)
    ff = w_ffn_in.shape[1] // 2
    tf = _ffn_tile(ff)
    chunked = lambda a: a.reshape(d, ff // tf, tf).transpose(1, 0, 2)
    bf = lambda a: a.astype(BF16)
    return dict(w_a=bf(w_a), w_b=bf(w_b), w_c=bf(conv), w_g=bf(gate), wuq=bf(wuq), wukv=bf(wukv),
                wfg=chunked(bf(w_ffn_in[:, :ff])), wfu=chunked(bf(w_ffn_in[:, ff:])))


def kernel(x, p, positions, w_in, b_gate, q_norm_g, w_uq, kv_norm_g, w_ukv, conv_w, conv_b, conv_ln_g,
           conv_ln_b, w_branch, w_out, ln1_g, ln1_b, w_ffn_in, w_ffn_out, w_ple_gate, w_ple_proj, ln2_g, ln2_b):
    batch, seq, d = x.shape
    depth = w_in.shape[0]
    T = batch * seq
    alpha = float((2 * depth) ** 0.25)
    tm = min(512, seq)
    tq_a = min(128, seq)
    tabs_a = _rope_tables(positions, A_HEAD_DIM, 0, A_HEAD_DIM)
    tabs_b = _rope_tables(positions, B_ROPE, B_NOPE, LANES)
    row = lambda v: v.reshape(1, -1)
    x2 = x.reshape(T, d)
    for i in range(depth):
        w = _prep_layer(w_in[i], w_uq[i], w_ukv[i], w_ffn_in[i])
        qa, qi, kk, vat, wit = _proj_a(x2, w["w_a"], tabs_a, tm)
        qb, kb, vbt = _proj_b(x2, w["w_b"], row(q_norm_g[i]), row(kv_norm_g[i]), w["wuq"], w["wukv"], tabs_b, tm)
        y_c = _conv_branch(x2, w["w_c"], conv_w[i], row(conv_b[i]), row(conv_ln_g[i]), row(conv_ln_b[i]),
                           batch, seq, tm)
        y_a = _dsa_attention(qa, qi, wit, kk, vat, batch, seq, tq_a, tm)
        y_b = _mla_attention(qb, kb, vbt, batch, seq, tm)
        x2 = _merge(x2, y_a, y_b, y_c, w["w_g"], b_gate[i], w_branch[i].astype(BF16), w_out[i].astype(BF16),
                    row(ln1_g[i]), row(ln1_b[i]), alpha, tm)
        tf = w["wfg"].shape[2]
        x2 = _ffn(x2, p[i].reshape(T, -1), w["wfg"], w["wfu"], w_ffn_out[i].astype(BF16).reshape(-1, tf, d),
                  w_ple_gate[i].astype(BF16), w_ple_proj[i].astype(BF16), row(ln2_g[i]), row(ln2_b[i]),
                  alpha, tm)
    return x2.reshape(batch, seq, d)
```

```python
import functools

import jax
import jax.numpy as jnp
from jax import lax
from jax.experimental import pallas as pl
from jax.experimental.pallas import tpu as pltpu

F32 = jnp.float32
BF16 = jnp.bfloat16
I32 = jnp.int32

LANES = 128
SUBLANES = 8

PLE_DIM = 256
A_HEADS = 8
A_HEAD_DIM = 64
IDX_HEADS = 4
IDX_DIM = 64
TOPK_MAX = 256
B_HEADS = 8
B_NOPE = 64
B_ROPE = 32
B_V = 64
Q_LORA = 384
KV_LORA = 256
C_CH = 512
CONV_W = 31
N_BRANCH = 3
ROPE_THETA = 10000.0
LN_EPS = 1e-5
RMS_EPS = 1e-6

INT_MIN = -(2 ** 31)
NEG_BIG = -1e30
LOG2E = 1.4426950408889634

VMEM_LIMIT = 56 * 1024 * 1024


def _cparams(sem):
    return pltpu.CompilerParams(dimension_semantics=sem, vmem_limit_bytes=VMEM_LIMIT)


def _layer_norm(v, g, b):
    mu = jnp.mean(v, axis=-1, keepdims=True)
    d = v - mu
    var = jnp.mean(d * d, axis=-1, keepdims=True)
    return d * lax.rsqrt(var + LN_EPS) * g + b


def _sigmoid(v):
    return 1.0 / (1.0 + jnp.exp(-v))


def _tree(op, parts):
    while len(parts) > 1:
        parts = [op(a, b) for a, b in zip(parts[::2], parts[1::2])] + ([parts[-1]] if len(parts) % 2 else [])
    return parts[0]


def _reduce_rows(op, v):
    slab = _tree(op, [v[r:r + SUBLANES] for r in range(0, v.shape[0], SUBLANES)])
    return op.reduce(slab, axis=0, keepdims=True)


def _rope_group(h, c, sa, sb, half):
    return h * c + pltpu.roll(h, LANES - half, 1) * sa + pltpu.roll(h, half, 1) * sb


def _proj_a_kernel(x_ref, w_ref, c_ref, sa_ref, sb_ref, qa_ref, qi_ref, kk_ref, va_ref, wi_ref):
    xb = x_ref[...].astype(BF16)
    h = jnp.dot(xb, w_ref[...], preferred_element_type=F32)
    c, sa, sb = c_ref[...], sa_ref[...], sb_ref[...]
    half = A_HEAD_DIM // 2
    for g in range(A_HEADS):
        hg = h[:, g * LANES:(g + 1) * LANES]
        qa_ref[:, g * LANES:(g + 1) * LANES] = _rope_group(hg, c, sa, sb, half).astype(BF16)
    off = A_HEADS
    for g in range(IDX_HEADS):
        hg = h[:, (off + g) * LANES:(off + g + 1) * LANES]
        qi_ref[:, g * LANES:(g + 1) * LANES] = _rope_group(hg, c, sa, sb, half).astype(BF16)
    off += IDX_HEADS
    hg = h[:, off * LANES:(off + 1) * LANES]
    kk_ref[...] = _rope_group(hg, c, sa, sb, half).astype(BF16)
    off += 1
    va_ref[0] = h[:, off * LANES:(off + 1) * LANES].T[:A_HEAD_DIM, :].astype(BF16)
    off += 1
    wi_ref[0] = h[:, off * LANES:(off + 1) * LANES].T[:8, :]


def _proj_a(x2, w_a, tabs, tm):
    T, D = x2.shape
    n = w_a.shape[1]
    row = lambda w: pl.BlockSpec((tm, w), lambda i: (i, 0))
    return pl.pallas_call(
        _proj_a_kernel,
        grid=(T // tm,),
        in_specs=[row(D), pl.BlockSpec((D, n), lambda i: (0, 0)), row(LANES), row(LANES), row(LANES)],
        out_specs=[row(A_HEADS * LANES), row(IDX_HEADS * LANES), row(LANES),
                   pl.BlockSpec((1, A_HEAD_DIM, tm), lambda i: (i, 0, 0)),
                   pl.BlockSpec((1, 8, tm), lambda i: (i, 0, 0))],
        out_shape=[jax.ShapeDtypeStruct((T, A_HEADS * LANES), BF16),
                   jax.ShapeDtypeStruct((T, IDX_HEADS * LANES), BF16),
                   jax.ShapeDtypeStruct((T, LANES), BF16),
                   jax.ShapeDtypeStruct((T // tm, A_HEAD_DIM, tm), BF16),
                   jax.ShapeDtypeStruct((T // tm, 8, tm), F32)],
        compiler_params=_cparams(("parallel",)),
        name="proj_a",
    )(x2, w_a, *tabs)


BIAS_LANE = B_NOPE + B_ROPE


def _proj_b_kernel(x_ref, w_ref, gq_ref, gkv_ref, wuq_ref, wukv_ref, c_ref, sa_ref, sb_ref,
                   q_ref, k_ref, v_ref, *, q_scale):
    xb = x_ref[...].astype(BF16)
    h = jnp.dot(xb, w_ref[...], preferred_element_type=F32)
    c, sa, sb = c_ref[...], sa_ref[...], sb_ref[...]
    half = B_ROPE // 2
    cq = h[:, :Q_LORA]
    cqn = cq * lax.rsqrt(jnp.mean(cq * cq, axis=-1, keepdims=True) + RMS_EPS) * gq_ref[...]
    ckv = h[:, Q_LORA:Q_LORA + KV_LORA]
    ckvn = ckv * lax.rsqrt(jnp.mean(ckv * ckv, axis=-1, keepdims=True) + RMS_EPS) * gkv_ref[...]
    kr = _rope_group(h[:, Q_LORA + KV_LORA:], c, sa, sb, half)
    q = jnp.dot(cqn.astype(BF16), wuq_ref[...], preferred_element_type=F32)
    kv = jnp.dot(ckvn.astype(BF16), wukv_ref[...], preferred_element_type=F32)
    one_hot = jnp.where(lax.broadcasted_iota(I32, (1, LANES), 1) == BIAS_LANE, 1.0, 0.0)
    for g in range(B_HEADS):
        qg = q[:, g * LANES:(g + 1) * LANES]
        q_ref[:, g * LANES:(g + 1) * LANES] = (_rope_group(qg, c, sa, sb, half) * q_scale + one_hot).astype(BF16)
        k_ref[:, g * LANES:(g + 1) * LANES] = (kv[:, g * LANES:(g + 1) * LANES] + kr).astype(BF16)
    v_ref[0] = kv[:, B_HEADS * LANES:].T.astype(BF16)


def _proj_b(x2, w_b, gq, gkv, wuq, wukv, tabs, tm):
    T, D = x2.shape
    row = lambda w: pl.BlockSpec((tm, w), lambda i: (i, 0))
    full = lambda a: pl.BlockSpec(a.shape, lambda i: (0, 0))
    q_scale = float((B_NOPE + B_ROPE) ** -0.5 * LOG2E)
    return pl.pallas_call(
        functools.partial(_proj_b_kernel, q_scale=q_scale),
        grid=(T // tm,),
        in_specs=[row(D), full(w_b), full(gq), full(gkv), full(wuq), full(wukv),
                  row(LANES), row(LANES), row(LANES)],
        out_specs=[row(B_HEADS * LANES), row(B_HEADS * LANES),
                   pl.BlockSpec((1, B_HEADS * B_V, tm), lambda i: (i, 0, 0))],
        out_shape=[jax.ShapeDtypeStruct((T, B_HEADS * LANES), BF16),
                   jax.ShapeDtypeStruct((T, B_HEADS * LANES), BF16),
                   jax.ShapeDtypeStruct((T // tm, B_HEADS * B_V, tm), BF16)],
        compiler_params=_cparams(("parallel",)),
        name="proj_b",
    )(x2, w_b, gq, gkv, wuq, wukv, *tabs)


CONV_HALO = 32
CONV_ROWS = 64


def _conv_kernel(x_ref, w_ref, cw_ref, cb_ref, g_ref, b_ref, o_ref, hbuf, *, tm):
    j = pl.program_id(1)
    span = CONV_HALO + tm

    @pl.when(j == 0)
    def _():
        hbuf[0, 0:CONV_HALO, :] = jnp.zeros((CONV_HALO, C_CH), F32)

    @pl.when(j > 0)
    def _():
        hbuf[0, 0:CONV_HALO, :] = hbuf[0, tm:span, :]

    xb = x_ref[...].astype(BF16)
    u = jnp.dot(xb, w_ref[...], preferred_element_type=F32)
    hbuf[0, CONV_HALO:span, :] = u[:, :C_CH] * _sigmoid(u[:, C_CH:])
    for r in range(1, SUBLANES):
        hbuf[r, SUBLANES:span, :] = hbuf[0, SUBLANES - r:span - r, :]
    for sb in range(tm // CONV_ROWS):
        r0 = sb * CONV_ROWS
        acc = jnp.zeros((CONV_ROWS, C_CH), F32) + cb_ref[...]
        for t in range(CONV_W):
            back = CONV_W - 1 - t
            r = back % SUBLANES
            s0 = CONV_HALO + r0 - back + r
            acc = acc + hbuf[r, s0:s0 + CONV_ROWS, :] * cw_ref[t:t + 1, :]
        y = _layer_norm(acc, g_ref[...], b_ref[...])
        o_ref[r0:r0 + CONV_ROWS, :] = (y * _sigmoid(y)).astype(BF16)


def _conv_branch(x2, w_c, cw, cb, g, b, batch, seq, tm):
    T, D = x2.shape
    nj = seq // tm
    row = lambda w: pl.BlockSpec((tm, w), lambda bi, j: (bi * nj + j, 0))
    full = lambda a: pl.BlockSpec(a.shape, lambda bi, j: (0, 0))
    return pl.pallas_call(
        functools.partial(_conv_kernel, tm=tm),
        grid=(batch, nj),
        in_specs=[row(D), full(w_c), full(cw), full(cb), full(g), full(b)],
        out_specs=row(C_CH),
        out_shape=jax.ShapeDtypeStruct((T, C_CH), BF16),
        scratch_shapes=[pltpu.VMEM((SUBLANES, tm + CONV_HALO, C_CH), F32)],
        compiler_params=_cparams(("arbitrary", "arbitrary")),
        name="conv_branch",
    )(x2, w_c, cw, cb, g, b)


WORD = 32
ONES_ROWS = 16


def _bit_planes(words):
    a = list(words)
    j, m = 16, 0x0000FFFF
    while j:
        k = 0
        while k < WORD:
            t = (a[k] ^ lax.shift_right_logical(a[k + j], jnp.int32(j))) & jnp.int32(m)
            a[k] = a[k] ^ t
            a[k + j] = a[k + j] ^ lax.shift_left(t, jnp.int32(j))
            k = (k + j + 1) & ~j
        j >>= 1
        m = (m ^ (m << j)) & 0x7FFFFFFF
    return a


def _dsa_kernel(qa_ref, qi_ref, wi_ref, kk_ref, vt_ref, o_ref, keys_ref, planes_ref, m_ref, acc_ref, sa_ref, sb_ref,
                *, tq, kc, top_k, idx_bits):
    i = pl.program_id(1)
    nkc = ((i + 1) * tq + kc - 1) // kc
    int_min = jnp.int32(INT_MIN)
    kthf = jnp.float32(top_k)
    nt = (((1,), (1,)), ((), ()))
    wrows = kc // WORD
    nwords = planes_ref.shape[1]
    nchunks = keys_ref.shape[0]

    qi = qi_ref[...]
    qi_st = jnp.concatenate([qi[:, h * LANES:(h + 1) * LANES] for h in range(IDX_HEADS)], axis=0)
    w = wi_ref[0]
    q_pos = i * tq + lax.broadcasted_iota(I32, (kc, tq), 1)
    k_iota = lax.broadcasted_iota(I32, (kc, tq), 0)

    @pl.when(i == 0)
    def _():
        planes_ref[...] = jnp.zeros(planes_ref.shape, I32)

    def score_chunk(c, diagonal):
        k0 = pl.multiple_of(c * kc, kc)
        kblk = kk_ref[pl.ds(k0, kc), :]
        lg = lax.dot_general(kblk, qi_st, nt, preferred_element_type=F32)
        sc = w[0:1, :] * jnp.maximum(lg[:, 0:tq], 0.0)
        for h in range(1, IDX_HEADS):
            sc = sc + w[h:h + 1, :] * jnp.maximum(lg[:, h * tq:(h + 1) * tq], 0.0)
        bits = lax.bitcast_convert_type(sc, I32)
        key = jnp.where(bits < 0, int_min - bits, bits)
        if diagonal:
            key = jnp.where(k0 + k_iota <= q_pos, key, int_min)
        keys_ref[c] = key
        w0 = pl.multiple_of(c * wrows, wrows)
        for g in range(wrows // SUBLANES):
            base = g * WORD * SUBLANES
            planes = _bit_planes([key[base + SUBLANES * j:base + SUBLANES * (j + 1)] for j in range(WORD)])
            planes[0] = ~planes[0]
            for s in range(WORD):
                planes_ref[s, pl.ds(w0 + SUBLANES * g, SUBLANES), :] = planes[s]

    def score_body(c, carry):
        score_chunk(c, False)
        return carry

    lax.fori_loop(0, nkc - 1, score_body, 0)
    score_chunk(nkc - 1, True)

    word_row = lax.broadcasted_iota(I32, (nwords, tq), 0)
    act0 = jnp.where(word_row < nkc * wrows, jnp.int32(-1), jnp.int32(0))

    def popcount_rows(v):
        cnt = _tree(jnp.add, [lax.population_count(v[r:r + SUBLANES]) for r in range(0, nwords, SUBLANES)])
        return jnp.sum(cnt.astype(F32), axis=0, keepdims=True)

    def bit_body(s, carry):
        act, rem, thr_u = carry
        cand = act & planes_ref[s]
        c1 = popcount_rows(cand)
        take = c1 >= rem
        act = jnp.where(take, cand, act ^ cand)
        rem = jnp.where(take, rem, rem - c1)
        thr_u = thr_u | jnp.where(take, lax.shift_left(jnp.int32(1), jnp.int32(31) - s), jnp.int32(0))
        return act, rem, thr_u

    act, rem, thr_u = lax.fori_loop(0, WORD, bit_body,
                                    (act0, jnp.full((1, tq), kthf, F32), jnp.zeros((1, tq), I32)))
    thr = thr_u ^ int_min

    need = jnp.logical_and(popcount_rows(act) > rem, thr > int_min)
    any_need = jnp.max(jnp.where(need, 1.0, 0.0)) > 0.0

    @pl.when(any_need)
    def _():
        row_index = (word_row & (SUBLANES - 1)) | lax.shift_left(lax.shift_right_logical(word_row, 3), 8)
        tied, left, j0 = act, rem, jnp.zeros((1, tq), I32)
        for b in reversed(range(idx_bits)):
            if 3 <= b < 8:
                pattern = sum(1 << j for j in range(WORD) if ((WORD - 1 - j) >> (b - 3)) & 1)
                plane = jnp.int32(pattern - 2 ** 32 if pattern >= 2 ** 31 else pattern)
            else:
                plane = -(lax.shift_right_logical(row_index, b) & 1)
            low = tied & ~plane
            n_low = popcount_rows(low)
            stay = n_low >= left
            tied = jnp.where(stay, low, tied & plane)
            left = jnp.where(stay, left, left - n_low)
            j0 = j0 | jnp.where(stay, jnp.int32(0), jnp.int32(1 << b))

        def drop_body(c, carry):
            kb = keys_ref[c]
            drop = jnp.logical_and(jnp.logical_and(need, kb == thr), c * kc + k_iota > j0)
            keys_ref[c] = jnp.where(drop, int_min, kb)
            return carry

        lax.fori_loop(0, nkc, drop_body, 0)

    thr_sel = jnp.maximum(thr, int_min + 1)

    q = qa_ref[...]
    eye = jnp.where(lax.broadcasted_iota(I32, (tq, tq), 0) == lax.broadcasted_iota(I32, (tq, tq), 1),
                    1.0, 0.0).astype(BF16)
    npairs = A_HEADS // 2
    q_pairs = [jnp.concatenate([jnp.concatenate([q[:, h * LANES:(h + 1) * LANES], eye], axis=1)
                                for h in (2 * hp, 2 * hp + 1)], axis=0) for hp in range(npairs)]
    ones = jnp.ones((ONES_ROWS, kc), BF16)
    m_ref[...] = jnp.full(m_ref.shape, NEG_BIG, F32)
    acc_ref[...] = jnp.zeros(acc_ref.shape, F32)

    def keys_aug(c):
        cc = jnp.minimum(c, nkc - 1)
        k0 = pl.multiple_of(cc * kc, kc)
        bias = jnp.where(keys_ref[cc] >= thr_sel, 0.0, NEG_BIG).astype(BF16)
        bias = jnp.where(c < nkc, bias, NEG_BIG)
        return jnp.concatenate([kk_ref[pl.ds(k0, kc), :], bias], axis=1)

    def scores(k_aug, hp, s_ref):
        cols = slice(hp * 2 * tq, (hp + 1) * 2 * tq)
        s_ref[:, cols] = lax.dot_general(k_aug, q_pairs[hp], nt, preferred_element_type=F32)

    def softmax_pv(vt_aug, hp, s_ref):
        cols = slice(hp * 2 * tq, (hp + 1) * 2 * tq)
        s = s_ref[:, cols]
        m_prev = m_ref[:, cols]
        m_new = jnp.maximum(m_prev, jnp.max(s, axis=0, keepdims=True))
        alpha = jnp.exp2(m_prev - m_new)
        p = jnp.exp2(s - m_new).astype(BF16)
        acc_ref[:, cols] = alpha * acc_ref[:, cols] + jnp.dot(vt_aug, p, preferred_element_type=F32)
        m_ref[:, cols] = m_new

    def half_trip(c, cur_ref, nxt_ref):
        k_aug = keys_aug(c + 1)
        vt_aug = jnp.concatenate([vt_ref[jnp.minimum(c, nchunks - 1)], ones], axis=0)
        for hp in range(npairs):
            scores(k_aug, hp, nxt_ref)
            softmax_pv(vt_aug, hp, cur_ref)

    def att_body(j, carry):
        half_trip(2 * j, sa_ref, sb_ref)
        half_trip(2 * j + 1, sb_ref, sa_ref)
        return carry

    k_aug0 = keys_aug(0)
    for hp in range(npairs):
        scores(k_aug0, hp, sa_ref)
    lax.fori_loop(0, (nkc + 1) // 2, att_body, 0)
    out = acc_ref[0:A_HEAD_DIM, :] / acc_ref[A_HEAD_DIM:A_HEAD_DIM + 1, :]
    for hp in range(npairs):
        pair = jnp.concatenate([out[:, (2 * hp) * tq:(2 * hp + 1) * tq],
                                out[:, (2 * hp + 1) * tq:(2 * hp + 2) * tq]], axis=0)
        o_ref[:, hp * LANES:(hp + 1) * LANES] = pair.T.astype(BF16)


def _dsa_attention(qa, qi, wit, kk, vat, batch, seq, tq, kc):
    T = qa.shape[0]
    nq = seq // tq
    r = kc // tq
    top_k = min(TOPK_MAX, seq // 4)
    qrow = lambda w: pl.BlockSpec((tq, w), lambda b, i: (b * nq + i, 0))
    return pl.pallas_call(
        functools.partial(_dsa_kernel, tq=tq, kc=kc, top_k=top_k, idx_bits=(seq - 1).bit_length()),
        grid=(batch, nq),
        in_specs=[qrow(A_HEADS * LANES), qrow(IDX_HEADS * LANES),
                  pl.BlockSpec((1, 8, tq), lambda b, i: ((b * nq + i) // r, 0, (b * nq + i) % r)),
                  pl.BlockSpec((seq, LANES), lambda b, i: (b, 0)),
                  pl.BlockSpec((seq // kc, A_HEAD_DIM, kc), lambda b, i: (b, 0, 0))],
        out_specs=qrow(A_HEADS * A_HEAD_DIM),
        out_shape=jax.ShapeDtypeStruct((T, A_HEADS * A_HEAD_DIM), BF16),
        scratch_shapes=[pltpu.VMEM((seq // kc, kc, tq), I32),
                        pltpu.VMEM((WORD, seq // WORD, tq), I32),
                        pltpu.VMEM((1, A_HEADS * tq), F32),
                        pltpu.VMEM((A_HEAD_DIM + ONES_ROWS, A_HEADS * tq), F32),
                        pltpu.VMEM((kc, A_HEADS * tq), F32),
                        pltpu.VMEM((kc, A_HEADS * tq), F32)],
        compiler_params=_cparams(("arbitrary", "arbitrary")),
        name="dsa_attention",
    )(qa, qi, wit, kk, vat)


def _mla_kernel(q_ref, k_ref, vt_ref, o_ref, m_ref, acc_ref, sa_ref, sb_ref, *, tq):
    i = pl.program_id(2)
    nq = vt_ref.shape[0]
    nt = (((1,), (1,)), ((), ()))
    k_pos = lax.broadcasted_iota(I32, (tq, tq), 0)
    q_pos = lax.broadcasted_iota(I32, (tq, tq), 1)
    bias_lane = lax.broadcasted_iota(I32, (1, LANES), 1) == BIAS_LANE
    ones = jnp.ones((ONES_ROWS, tq), BF16)
    m_ref[...] = jnp.full(m_ref.shape, NEG_BIG, F32)
    acc_ref[...] = jnp.zeros(acc_ref.shape, F32)

    def scores(c, hh, s_ref, valid):
        k0 = pl.multiple_of(jnp.minimum(c, nq - 1) * tq, tq)
        kblk = k_ref[pl.ds(k0, tq), hh * LANES:(hh + 1) * LANES]
        if valid is not None:
            off = jnp.where(jnp.logical_and(bias_lane, jnp.logical_not(valid)), NEG_BIG, 0.0)
            kblk = kblk + off.astype(BF16)
        s_ref[hh] = lax.dot_general(kblk, q_ref[:, hh * LANES:(hh + 1) * LANES], nt,
                                    preferred_element_type=F32)

    def softmax_pv(c, hh, s_ref, diagonal):
        vt_aug = jnp.concatenate([vt_ref[jnp.minimum(c, nq - 1), hh * B_V:(hh + 1) * B_V, :], ones], axis=0)
        s = s_ref[hh]
        if diagonal:
            s = jnp.where(k_pos <= q_pos, s, NEG_BIG)
        m_prev = m_ref[hh]
        m_new = jnp.maximum(m_prev, jnp.max(s, axis=0, keepdims=True))
        alpha = jnp.exp2(m_prev - m_new)
        p = jnp.exp2(s - m_new).astype(BF16)
        acc_ref[hh] = alpha * acc_ref[hh] + jnp.dot(vt_aug, p, preferred_element_type=F32)
        m_ref[hh] = m_new

    def half_trip(cur, cur_ref, nxt, nxt_ref, diagonal=False):
        for hh in range(2):
            scores(nxt, hh, nxt_ref, nxt < i)
            softmax_pv(cur, hh, cur_ref, diagonal)

    for hh in range(2):
        scores(i, hh, sa_ref, None)
    half_trip(i, sa_ref, 0, sb_ref, diagonal=True)

    def trip(t, carry):
        half_trip(2 * t, sb_ref, 2 * t + 1, sa_ref)
        half_trip(2 * t + 1, sa_ref, 2 * t + 2, sb_ref)
        return carry

    lax.fori_loop(0, (i + 1) // 2, trip, 0)
    outs = [acc_ref[hh, 0:B_V, :] / acc_ref[hh, B_V:B_V + 1, :] for hh in range(2)]
    o_ref[...] = jnp.concatenate(outs, axis=0).T.astype(BF16)


def _mla_attention(q, k, vt, batch, seq, tq):
    T = q.shape[0]
    nq = seq // tq
    pairs = B_HEADS // 2
    return pl.pallas_call(
        functools.partial(_mla_kernel, tq=tq),
        grid=(batch, pairs, nq),
        in_specs=[pl.BlockSpec((tq, 2 * LANES), lambda b, h, i: (b * nq + i, h)),
                  pl.BlockSpec((seq, 2 * LANES), lambda b, h, i: (b, h)),
                  pl.BlockSpec((nq, 2 * B_V, tq), lambda b, h, i: (b, h, 0))],
        out_specs=pl.BlockSpec((tq, 2 * B_V), lambda b, h, i: (b * nq + i, h)),
        out_shape=jax.ShapeDtypeStruct((T, B_HEADS * B_V), BF16),
        scratch_shapes=[pltpu.VMEM((2, 1, tq), F32), pltpu.VMEM((2, B_V + ONES_ROWS, tq), F32),
                        pltpu.VMEM((2, tq, tq), F32), pltpu.VMEM((2, tq, tq), F32)],
        compiler_params=_cparams(("parallel", "parallel", "arbitrary")),
        name="mla_attention",
    )(q, k, vt)


def _merge_kernel(x_ref, ya_ref, yb_ref, yc_ref, wg_ref, bg_ref, wbr_ref, wo_ref, g_ref, b_ref, o_ref,
                  *, alpha, d):
    x = x_ref[...]
    xb = x.astype(BF16)
    merged = None
    for k, y_ref in enumerate((ya_ref, yb_ref, yc_ref)):
        pre = jnp.dot(xb, wg_ref[:, k * d:(k + 1) * d], preferred_element_type=F32) + bg_ref[k:k + 1, :]
        br = jnp.dot(y_ref[...], wbr_ref[k], preferred_element_type=F32)
        term = _sigmoid(pre) * br
        merged = term if merged is None else merged + term
    mix = jnp.dot(merged.astype(BF16), wo_ref[...], preferred_element_type=F32)
    o_ref[...] = _layer_norm(alpha * x + mix, g_ref[...], b_ref[...])


def _merge(x2, ya, yb, yc, wg, bg, wbr, wo, g, b, alpha, tm):
    T, D = x2.shape
    row = lambda w: pl.BlockSpec((tm, w), lambda i: (i, 0))
    full = lambda a: pl.BlockSpec(a.shape, lambda i: (0,) * a.ndim, pipeline_mode=pl.Buffered(1))
    return pl.pallas_call(
        functools.partial(_merge_kernel, alpha=alpha, d=D),
        grid=(T // tm,),
        in_specs=[row(D), row(ya.shape[1]), row(yb.shape[1]), row(yc.shape[1]),
                  full(wg), full(bg), full(wbr), full(wo), full(g), full(b)],
        out_specs=row(D),
        out_shape=jax.ShapeDtypeStruct((T, D), F32),
        compiler_params=_cparams(("parallel",)),
        name="merge_ln1",
    )(x2, ya, yb, yc, wg, bg, wbr, wo, g, b)


def _ffn_kernel(x_ref, p_ref, wg_ref, wu_ref, wo_ref, wpg_ref, wpp_ref, g_ref, b_ref, o_ref, *, alpha):
    x = x_ref[...]
    xb = x.astype(BF16)
    nchunk = wg_ref.shape[0]

    def hidden(c):
        fg = jnp.dot(xb, wg_ref[c], preferred_element_type=F32)
        fu = jnp.dot(xb, wu_ref[c], preferred_element_type=F32)
        return (fg * _sigmoid(fg) * fu).astype(BF16)

    gate = _sigmoid(jnp.dot(xb, wpg_ref[...], preferred_element_type=F32))
    emb = jnp.dot(p_ref[...].astype(BF16), wpp_ref[...], preferred_element_type=F32)
    acc = alpha * x + gate * emb
    a_prev = hidden(0)
    for c in range(1, nchunk):
        a_cur = hidden(c)
        acc = acc + jnp.dot(a_prev, wo_ref[c - 1], preferred_element_type=F32)
        a_prev = a_cur
    acc = acc + jnp.dot(a_prev, wo_ref[nchunk - 1], preferred_element_type=F32)
    o_ref[...] = _layer_norm(acc, g_ref[...], b_ref[...])


def _ffn(x2, p2, wg, wu, wo, wpg, wpp, g, b, alpha, tm):
    T, D = x2.shape
    row = lambda w: pl.BlockSpec((tm, w), lambda i: (i, 0))
    full = lambda a: pl.BlockSpec(a.shape, lambda i: (0,) * a.ndim, pipeline_mode=pl.Buffered(1))
    return pl.pallas_call(
        functools.partial(_ffn_kernel, alpha=alpha),
        grid=(T // tm,),
        in_specs=[row(D), row(p2.shape[1]), full(wg), full(wu), full(wo), full(wpg), full(wpp), full(g), full(b)],
        out_specs=row(D),
        out_shape=jax.ShapeDtypeStruct((T, D), F32),
        compiler_params=_cparams(("parallel",)),
        name="ffn_ple_ln2",
    )(x2, p2, wg, wu, wo, wpg, wpp, g, b)


def _rope_tables(positions, dim, lane_lo, period):
    half = dim // 2
    inv = 1.0 / (ROPE_THETA ** (jnp.arange(0, dim, 2, dtype=F32) / dim))
    ang = positions.reshape(-1).astype(F32)[:, None] * inv
    cos, sin = jnp.cos(ang), jnp.sin(ang)
    lane = jnp.arange(LANES)
    rel = lane % period - lane_lo
    inside = (rel >= 0) & (rel < dim)
    f = jnp.where(inside, rel % half, 0)
    lo = inside & (rel < half)
    hi = inside & (rel >= half)
    c = jnp.where(inside[None], cos[:, f], 1.0)
    sa = jnp.where(lo[None], -sin[:, f], 0.0)
    sb = jnp.where(hi[None], sin[:, f], 0.0)
    return c, sa, sb


def _pad_cols(w, n):
    return jnp.pad(w, ((0, 0), (0, n - w.shape[1])))


def _ffn_tile(ff):
    for tf in (512, 256, 128):
        if ff % tf == 0:
            return tf
    return ff


def _prep_layer(w_in, w_uq, w_ukv, w_ffn_in):
    d = w_in.shape[0]
    sizes = (A_HEADS * A_HEAD_DIM, A_HEAD_DIM, A_HEAD_DIM, IDX_HEADS * IDX_DIM, IDX_DIM, IDX_HEADS,
             Q_LORA, KV_LORA, B_ROPE, 2 * C_CH, N_BRANCH * d)
    splits = [sum(sizes[:k + 1]) for k in range(len(sizes) - 1)]
    qa, ka, va, qi, ki, wi, cq, ckv, kr, conv, gate = jnp.split(w_in, splits, axis=1)
    qa = jnp.pad((qa * (A_HEAD_DIM ** -0.5 * LOG2E)).reshape(d, A_HEADS, A_HEAD_DIM),
                 ((0, 0), (0, 0), (0, LANES - A_HEAD_DIM))).reshape(d, A_HEADS * LANES)
    qi = jnp.pad(qi.reshape(d, IDX_HEADS, IDX_DIM),
                 ((0, 0), (0, 0), (LANES - IDX_DIM, 0))).reshape(d, IDX_HEADS * LANES)
    wi = wi * (IDX_DIM ** -0.5 * IDX_HEADS ** -0.5)
    w_a = jnp.concatenate([qa, qi, ka, ki, _pad_cols(va, LANES), _pad_cols(wi, LANES)], axis=1)
    kr = jnp.pad(kr, ((0, 0), (B_NOPE, LANES - B_NOPE - B_ROPE)))
    w_b = jnp.concatenate([cq, ckv, kr], axis=1)
    wuq = jnp.pad(w_uq.reshape(-1, B_HEADS, B_NOPE + B_ROPE),
                  ((0, 0), (0, 0), (0, LANES - B_NOPE - B_ROPE))).reshape(-1, B_HEADS * LANES)
    ukv = w_ukv.reshape(-1, B_HEADS, B_NOPE + B_V)
    wuk = jnp.pad(ukv[:, :, :B_NOPE], ((0, 0), (0, 0), (0, LANES - B_NOPE))).reshape(-1, B_HEADS * LANES)
    wuv = ukv[:, :, B_NOPE:].reshape(-1, B_HEADS * B_V)
    wukv = jnp.concatenate([wuk, wuv], axis=1)
    ff = w_ffn_in.shape[1] // 2
    tf = _ffn_tile(ff)
    chunked = lambda a: a.reshape(d, ff // tf, tf).transpose(1, 0, 2)
    bf = lambda a: a.astype(BF16)
    return dict(w_a=bf(w_a), w_b=bf(w_b), w_c=bf(conv), w_g=bf(gate), wuq=bf(wuq), wukv=bf(wukv),
                wfg=chunked(bf(w_ffn_in[:, :ff])), wfu=chunked(bf(w_ffn_in[:, ff:])))


def kernel(x, p, positions, w_in, b_gate, q_norm_g, w_uq, kv_norm_g, w_ukv, conv_w, conv_b, conv_ln_g,
           conv_ln_b, w_branch, w_out, ln1_g, ln1_b, w_ffn_in, w_ffn_out, w_ple_gate, w_ple_proj, ln2_g, ln2_b):
    batch, seq, d = x.shape
    depth = w_in.shape[0]
    T = batch * seq
    alpha = float((2 * depth) ** 0.25)
    tm = min(512, seq)
    tq_a = min(128, seq)
    tabs_a = _rope_tables(positions, A_HEAD_DIM, 0, A_HEAD_DIM)
    tabs_b = _rope_tables(positions, B_ROPE, B_NOPE, LANES)
    row = lambda v: v.reshape(1, -1)
    x2 = x.reshape(T, d)
    for i in range(depth):
        w = _prep_layer(w_in[i], w_uq[i], w_ukv[i], w_ffn_in[i])
        qa, qi, kk, vat, wit = _proj_a(x2, w["w_a"], tabs_a, tm)
        qb, kb, vbt = _proj_b(x2, w["w_b"], row(q_norm_g[i]), row(kv_norm_g[i]), w["wuq"], w["wukv"], tabs_b, tm)
        y_c = _conv_branch(x2, w["w_c"], conv_w[i], row(conv_b[i]), row(conv_ln_g[i]), row(conv_ln_b[i]),
                           batch, seq, tm)
        y_a = _dsa_attention(qa, qi, wit, kk, vat, batch, seq, tq_a, tm)
        y_b = _mla_attention(qb, kb, vbt, batch, seq, tm)
        x2 = _merge(x2, y_a, y_b, y_c, w["w_g"], b_gate[i], w_branch[i].astype(BF16), w_out[i].astype(BF16),
                    row(ln1_g[i]), row(ln1_b[i]), alpha, tm)
        tf = w["wfg"].shape[2]
        x2 = _ffn(x2, p[i].reshape(T, -1), w["wfg"], w["wfu"], w_ffn_out[i].astype(BF16).reshape(-1, tf, d),
                  w_ple_gate[i].astype(BF16), w_ple_proj[i].astype(BF16), row(ln2_g[i]), row(ln2_b[i]),
                  alpha, tm)
    return x2.reshape(batch, seq, d)
```

```python
import functools

import jax
import jax.numpy as jnp
from jax import lax
from jax.experimental import pallas as pl
from jax.experimental.pallas import tpu as pltpu

F32 = jnp.float32
BF16 = jnp.bfloat16
I32 = jnp.int32

LANES = 128
SUBLANES = 8

PLE_DIM = 256
A_HEADS = 8
A_HEAD_DIM = 64
IDX_HEADS = 4
IDX_DIM = 64
TOPK_MAX = 256
B_HEADS = 8
B_NOPE = 64
B_ROPE = 32
B_V = 64
Q_LORA = 384
KV_LORA = 256
C_CH = 512
CONV_W = 31
N_BRANCH = 3
ROPE_THETA = 10000.0
LN_EPS = 1e-5
RMS_EPS = 1e-6

INT_MIN = -(2 ** 31)
NEG_BIG = -1e30
LOG2E = 1.4426950408889634

VMEM_LIMIT = 56 * 1024 * 1024


def _cparams(sem):
    return pltpu.CompilerParams(dimension_semantics=sem, vmem_limit_bytes=VMEM_LIMIT)


def _layer_norm(v, g, b):
    mu = jnp.mean(v, axis=-1, keepdims=True)
    d = v - mu
    var = jnp.mean(d * d, axis=-1, keepdims=True)
    return d * lax.rsqrt(var + LN_EPS) * g + b


def _sigmoid(v):
    return 1.0 / (1.0 + jnp.exp(-v))


def _tree(op, parts):
    while len(parts) > 1:
        parts = [op(a, b) for a, b in zip(parts[::2], parts[1::2])] + ([parts[-1]] if len(parts) % 2 else [])
    return parts[0]


def _reduce_rows(op, v):
    slab = _tree(op, [v[r:r + SUBLANES] for r in range(0, v.shape[0], SUBLANES)])
    return op.reduce(slab, axis=0, keepdims=True)


def _rope_group(h, c, sa, sb, half):
    return h * c + pltpu.roll(h, LANES - half, 1) * sa + pltpu.roll(h, half, 1) * sb


def _proj_a_kernel(x_ref, w_ref, c_ref, sa_ref, sb_ref, qa_ref, qi_ref, kk_ref, va_ref, wi_ref):
    xb = x_ref[...].astype(BF16)
    h = jnp.dot(xb, w_ref[...], preferred_element_type=F32)
    c, sa, sb = c_ref[...], sa_ref[...], sb_ref[...]
    half = A_HEAD_DIM // 2
    for g in range(A_HEADS):
        hg = h[:, g * LANES:(g + 1) * LANES]
        qa_ref[:, g * LANES:(g + 1) * LANES] = _rope_group(hg, c, sa, sb, half).astype(BF16)
    off = A_HEADS
    for g in range(IDX_HEADS):
        hg = h[:, (off + g) * LANES:(off + g + 1) * LANES]
        qi_ref[:, g * LANES:(g + 1) * LANES] = _rope_group(hg, c, sa, sb, half).astype(BF16)
    off += IDX_HEADS
    hg = h[:, off * LANES:(off + 1) * LANES]
    kk_ref[...] = _rope_group(hg, c, sa, sb, half).astype(BF16)
    off += 1
    va_ref[0] = h[:, off * LANES:(off + 1) * LANES].T[:A_HEAD_DIM, :].astype(BF16)
    off += 1
    wi_ref[0] = h[:, off * LANES:(off + 1) * LANES].T[:SUBLANES, :]


def _proj_a(x2, w_a, tabs, tm):
    T, D = x2.shape
    n = w_a.shape[1]
    row = lambda w: pl.BlockSpec((tm, w), lambda i: (i, 0))
    return pl.pallas_call(
        _proj_a_kernel,
        grid=(T // tm,),
        in_specs=[row(D), pl.BlockSpec((D, n), lambda i: (0, 0)), row(LANES), row(LANES), row(LANES)],
        out_specs=[row(A_HEADS * LANES), row(IDX_HEADS * LANES), row(LANES),
                   pl.BlockSpec((1, A_HEAD_DIM, tm), lambda i: (i, 0, 0)),
                   pl.BlockSpec((1, SUBLANES, tm), lambda i: (i, 0, 0))],
        out_shape=[jax.ShapeDtypeStruct((T, A_HEADS * LANES), BF16),
                   jax.ShapeDtypeStruct((T, IDX_HEADS * LANES), BF16),
                   jax.ShapeDtypeStruct((T, LANES), BF16),
                   jax.ShapeDtypeStruct((T // tm, A_HEAD_DIM, tm), BF16),
                   jax.ShapeDtypeStruct((T // tm, SUBLANES, tm), F32)],
        compiler_params=_cparams(("parallel",)),
        name="proj_a",
    )(x2, w_a, *tabs)


BIAS_LANE = B_NOPE + B_ROPE


def _proj_b_kernel(x_ref, w_ref, gq_ref, gkv_ref, wuq_ref, wukv_ref, c_ref, sa_ref, sb_ref,
                   q_ref, k_ref, v_ref, *, q_scale):
    xb = x_ref[...].astype(BF16)
    h = jnp.dot(xb, w_ref[...], preferred_element_type=F32)
    c, sa, sb = c_ref[...], sa_ref[...], sb_ref[...]
    half = B_ROPE // 2
    cq = h[:, :Q_LORA]
    cqn = cq * lax.rsqrt(jnp.mean(cq * cq, axis=-1, keepdims=True) + RMS_EPS) * gq_ref[...]
    ckv = h[:, Q_LORA:Q_LORA + KV_LORA]
    ckvn = ckv * lax.rsqrt(jnp.mean(ckv * ckv, axis=-1, keepdims=True) + RMS_EPS) * gkv_ref[...]
    kr = _rope_group(h[:, Q_LORA + KV_LORA:], c, sa, sb, half)
    q = jnp.dot(cqn.astype(BF16), wuq_ref[...], preferred_element_type=F32)
    kv = jnp.dot(ckvn.astype(BF16), wukv_ref[...], preferred_element_type=F32)
    one_hot = jnp.where(lax.broadcasted_iota(I32, (1, LANES), 1) == BIAS_LANE, 1.0, 0.0)
    for g in range(B_HEADS):
        qg = q[:, g * LANES:(g + 1) * LANES]
        q_ref[:, g * LANES:(g + 1) * LANES] = (_rope_group(qg, c, sa, sb, half) * q_scale + one_hot).astype(BF16)
        k_ref[:, g * LANES:(g + 1) * LANES] = (kv[:, g * LANES:(g + 1) * LANES] + kr).astype(BF16)
    v_ref[0] = kv[:, B_HEADS * LANES:].T.astype(BF16)


def _proj_b(x2, w_b, gq, gkv, wuq, wukv, tabs, tm):
    T, D = x2.shape
    row = lambda w: pl.BlockSpec((tm, w), lambda i: (i, 0))
    full = lambda a: pl.BlockSpec(a.shape, lambda i: (0, 0))
    q_scale = float((B_NOPE + B_ROPE) ** -0.5 * LOG2E)
    return pl.pallas_call(
        functools.partial(_proj_b_kernel, q_scale=q_scale),
        grid=(T // tm,),
        in_specs=[row(D), full(w_b), full(gq), full(gkv), full(wuq), full(wukv),
                  row(LANES), row(LANES), row(LANES)],
        out_specs=[row(B_HEADS * LANES), row(B_HEADS * LANES),
                   pl.BlockSpec((1, B_HEADS * B_V, tm), lambda i: (i, 0, 0))],
        out_shape=[jax.ShapeDtypeStruct((T, B_HEADS * LANES), BF16),
                   jax.ShapeDtypeStruct((T, B_HEADS * LANES), BF16),
                   jax.ShapeDtypeStruct((T // tm, B_HEADS * B_V, tm), BF16)],
        compiler_params=_cparams(("parallel",)),
        name="proj_b",
    )(x2, w_b, gq, gkv, wuq, wukv, *tabs)


CONV_HALO = 32
CONV_ROWS = 64


def _conv_kernel(x_ref, w_ref, cw_ref, cb_ref, g_ref, b_ref, o_ref, hbuf, *, tm):
    j = pl.program_id(1)
    span = CONV_HALO + tm

    @pl.when(j == 0)
    def _():
        hbuf[0, 0:CONV_HALO, :] = jnp.zeros((CONV_HALO, C_CH), F32)

    @pl.when(j > 0)
    def _():
        hbuf[0, 0:CONV_HALO, :] = hbuf[0, tm:span, :]

    xb = x_ref[...].astype(BF16)
    u = jnp.dot(xb, w_ref[...], preferred_element_type=F32)
    hbuf[0, CONV_HALO:span, :] = u[:, :C_CH] * _sigmoid(u[:, C_CH:])
    for r in range(1, SUBLANES):
        hbuf[r, SUBLANES:span, :] = hbuf[0, SUBLANES - r:span - r, :]
    for sb in range(tm // CONV_ROWS):
        r0 = sb * CONV_ROWS
        acc = jnp.zeros((CONV_ROWS, C_CH), F32) + cb_ref[...]
        for t in range(CONV_W):
            back = CONV_W - 1 - t
            r = back % SUBLANES
            s0 = CONV_HALO + r0 - back + r
            acc = acc + hbuf[r, s0:s0 + CONV_ROWS, :] * cw_ref[t:t + 1, :]
        y = _layer_norm(acc, g_ref[...], b_ref[...])
        o_ref[r0:r0 + CONV_ROWS, :] = (y * _sigmoid(y)).astype(BF16)


def _conv_branch(x2, w_c, cw, cb, g, b, batch, seq, tm):
    T, D = x2.shape
    nj = seq // tm
    row = lambda w: pl.BlockSpec((tm, w), lambda bi, j: (bi * nj + j, 0))
    full = lambda a: pl.BlockSpec(a.shape, lambda bi, j: (0, 0))
    return pl.pallas_call(
        functools.partial(_conv_kernel, tm=tm),
        grid=(batch, nj),
        in_specs=[row(D), full(w_c), full(cw), full(cb), full(g), full(b)],
        out_specs=row(C_CH),
        out_shape=jax.ShapeDtypeStruct((T, C_CH), BF16),
        scratch_shapes=[pltpu.VMEM((SUBLANES, tm + CONV_HALO, C_CH), F32)],
        compiler_params=_cparams(("arbitrary", "arbitrary")),
        name="conv_branch",
    )(x2, w_c, cw, cb, g, b)


WORD = 32
ONES_ROWS = 16


def _bit_planes(words):
    a = list(words)
    masks = {16: 0x0000FFFF, 8: 0x00FF00FF, 4: 0x0F0F0F0F, 2: 0x33333333, 1: 0x55555555}

    def swap(k, j):
        t = (a[k] ^ lax.shift_right_logical(a[k + j], jnp.int32(j))) & jnp.int32(masks[j])
        a[k] = a[k] ^ t
        a[k + j] = a[k + j] ^ lax.shift_left(t, jnp.int32(j))

    for base in range(0, WORD, 8):
        for j in (4, 2, 1):
            for k in range(base, base + 8):
                if not k & j:
                    swap(k, j)
    for r in range(8):
        for j in (16, 8):
            for k in range(r, WORD, 8):
                if not k & j:
                    swap(k, j)
    return a


def _dsa_kernel(qa_ref, qi_ref, wi_ref, kk_ref, vt_ref, o_ref, keys_ref, planes_ref, m_ref, acc_ref, sa_ref, sb_ref,
                *, tq, kc, top_k, idx_bits):
    i = pl.program_id(1)
    nkc = ((i + 1) * tq + kc - 1) // kc
    int_min = jnp.int32(INT_MIN)
    kthf = jnp.float32(top_k)
    nt = (((1,), (1,)), ((), ()))
    wrows = kc // WORD
    nwords = planes_ref.shape[1]
    nchunks = keys_ref.shape[0]

    qi = qi_ref[...]
    qi_st = jnp.concatenate([qi[:, h * LANES:(h + 1) * LANES] for h in range(IDX_HEADS)], axis=0)
    w = wi_ref[0]
    q_pos = i * tq + lax.broadcasted_iota(I32, (kc, tq), 1)
    k_iota = lax.broadcasted_iota(I32, (kc, tq), 0)

    @pl.when(i == 0)
    def _():
        planes_ref[...] = jnp.zeros(planes_ref.shape, I32)

    def score_chunk(c, diagonal):
        k0 = pl.multiple_of(c * kc, kc)
        kblk = kk_ref[pl.ds(k0, kc), :]
        lg = lax.dot_general(kblk, qi_st, nt, preferred_element_type=F32)
        sc = w[0:1, :] * jnp.maximum(lg[:, 0:tq], 0.0)
        for h in range(1, IDX_HEADS):
            sc = sc + w[h:h + 1, :] * jnp.maximum(lg[:, h * tq:(h + 1) * tq], 0.0)
        bits = lax.bitcast_convert_type(sc, I32)
        key = jnp.where(bits < 0, int_min - bits, bits)
        if diagonal:
            key = jnp.where(k0 + k_iota <= q_pos, key, int_min)
        keys_ref[c] = key
        w0 = pl.multiple_of(c * wrows, wrows)
        for g in range(wrows // SUBLANES):
            base = g * WORD * SUBLANES
            planes = _bit_planes([key[base + SUBLANES * j:base + SUBLANES * (j + 1)] for j in range(WORD)])
            planes[0] = ~planes[0]
            for s in range(WORD):
                planes_ref[s, pl.ds(w0 + SUBLANES * g, SUBLANES), :] = planes[s]

    def score_body(c, carry):
        score_chunk(c, False)
        return carry

    lax.fori_loop(0, nkc - 1, score_body, 0)
    score_chunk(nkc - 1, True)

    word_row = lax.broadcasted_iota(I32, (nwords, tq), 0)
    act0 = jnp.where(word_row < nkc * wrows, jnp.int32(-1), jnp.int32(0))

    def popcount_rows(v):
        cnt = _tree(jnp.add, [lax.population_count(v[r:r + SUBLANES]) for r in range(0, nwords, SUBLANES)])
        return jnp.sum(cnt.astype(F32), axis=0, keepdims=True)

    def bit_body(s, carry):
        act, rem, thr_u = carry
        cand = act & planes_ref[s]
        c1 = popcount_rows(cand)
        take = c1 >= rem
        act = jnp.where(take, cand, act ^ cand)
        rem = jnp.where(take, rem, rem - c1)
        thr_u = thr_u | jnp.where(take, lax.shift_left(jnp.int32(1), jnp.int32(31) - s), jnp.int32(0))
        return act, rem, thr_u

    act, rem, thr_u = lax.fori_loop(0, WORD, bit_body,
                                    (act0, jnp.full((1, tq), kthf, F32), jnp.zeros((1, tq), I32)))
    thr = thr_u ^ int_min

    need = jnp.logical_and(popcount_rows(act) > rem, thr > int_min)
    any_need = jnp.max(jnp.where(need, 1.0, 0.0)) > 0.0

    def last_tied_index():
        sub_bits = SUBLANES.bit_length() - 1
        group_bits = (WORD * SUBLANES).bit_length() - 1
        row_index = (word_row & (SUBLANES - 1)) | lax.shift_left(lax.shift_right_logical(word_row, sub_bits),
                                                                 group_bits)
        tied, left, j0 = act, rem, jnp.zeros((1, tq), I32)
        for b in reversed(range(idx_bits)):
            if sub_bits <= b < group_bits:
                pattern = sum(1 << j for j in range(WORD) if ((WORD - 1 - j) >> (b - sub_bits)) & 1)
                plane = jnp.int32(pattern - 2 ** 32 if pattern >= 2 ** 31 else pattern)
            else:
                plane = -(lax.shift_right_logical(row_index, b) & 1)
            low = tied & ~plane
            n_low = popcount_rows(low)
            stay = n_low >= left
            tied = jnp.where(stay, low, tied & plane)
            left = jnp.where(stay, left, left - n_low)
            j0 = j0 | jnp.where(stay, jnp.int32(0), jnp.int32(1 << b))
        return j0

    j0 = lax.cond(any_need, last_tied_index, lambda: jnp.zeros((1, tq), I32))
    j0 = jnp.where(need, j0, jnp.where(thr > int_min, jnp.int32(2 ** 31 - 1), jnp.int32(-1)))

    q = qa_ref[...]
    sub = LANES
    eye = jnp.where(lax.broadcasted_iota(I32, (sub, sub), 0) == lax.broadcasted_iota(I32, (sub, sub), 1),
                    1.0, 0.0).astype(BF16)
    npairs = A_HEADS // 2
    units = [(u, hp) for u in range(tq // sub) for hp in range(npairs)]
    q_unit = [jnp.concatenate([jnp.concatenate([q[u * sub:(u + 1) * sub, h * LANES:(h + 1) * LANES], eye], axis=1)
                               for h in (2 * hp, 2 * hp + 1)], axis=0) for u, hp in units]
    cols = [slice(n * 2 * sub, (n + 1) * 2 * sub) for n in range(len(units))]
    ones = jnp.ones((ONES_ROWS, kc), BF16)
    m_ref[...] = jnp.full(m_ref.shape, NEG_BIG, F32)
    acc_ref[...] = jnp.zeros(acc_ref.shape, F32)

    def keys_aug(c):
        cc = jnp.minimum(c, nkc - 1)
        k0 = pl.multiple_of(cc * kc, kc)
        kb = keys_ref[cc]
        tie_bias = jnp.where(k0 + k_iota <= j0, 0.0, NEG_BIG)
        bias = jnp.where(kb > thr, 0.0, jnp.where(kb == thr, tie_bias, NEG_BIG)).astype(BF16)
        bias = jnp.where(c < nkc, bias, NEG_BIG)
        kblk = kk_ref[pl.ds(k0, kc), :]
        return [jnp.concatenate([kblk, bias[:, u * sub:(u + 1) * sub]], axis=1)
                for u in range(tq // sub)]

    def scores(k_aug, n, s_ref):
        s_ref[:, cols[n]] = lax.dot_general(k_aug[units[n][0]], q_unit[n], nt,
                                            preferred_element_type=F32)

    def softmax(n, s_ref):
        s = s_ref[:, cols[n]]
        m_prev = m_ref[:, cols[n]]
        m_new = jnp.maximum(m_prev, jnp.max(s, axis=0, keepdims=True))
        m_ref[:, cols[n]] = m_new
        return jnp.exp2(m_prev - m_new), jnp.exp2(s - m_new).astype(BF16)

    def weighted_values(vt_aug, n, alpha, p):
        acc_ref[:, cols[n]] = alpha * acc_ref[:, cols[n]] + jnp.dot(vt_aug, p, preferred_element_type=F32)

    def half_trip(c, cur_ref, nxt_ref):
        k_aug = keys_aug(c + 1)
        vt_aug = jnp.concatenate([vt_ref[jnp.minimum(c, nchunks - 1)], ones], axis=0)
        pending = None
        for n in range(len(units)):
            scores(k_aug, n, nxt_ref)
            if pending is not None:
                weighted_values(vt_aug, n - 1, *pending)
            pending = softmax(n, cur_ref)
        weighted_values(vt_aug, len(units) - 1, *pending)

    def att_body(j, carry):
        half_trip(2 * j, sa_ref, sb_ref)
        half_trip(2 * j + 1, sb_ref, sa_ref)
        return carry

    k_aug0 = keys_aug(0)
    for n in range(len(units)):
        scores(k_aug0, n, sa_ref)
    lax.fori_loop(0, (nkc + 1) // 2, att_body, 0)
    out = acc_ref[0:A_HEAD_DIM, :] / acc_ref[A_HEAD_DIM:A_HEAD_DIM + 1, :]
    for n, (u, hp) in enumerate(units):
        o2 = out[:, cols[n]]
        pair = jnp.concatenate([o2[:, :sub], o2[:, sub:]], axis=0)
        o_ref[u * sub:(u + 1) * sub, hp * LANES:(hp + 1) * LANES] = pair.T.astype(BF16)


def _dsa_attention(qa, qi, wit, kk, vat, batch, seq, tq, kc):
    T = qa.shape[0]
    nq = seq // tq
    r = kc // tq
    top_k = min(TOPK_MAX, seq // 4)
    qrow = lambda w: pl.BlockSpec((tq, w), lambda b, i: (b * nq + i, 0))
    return pl.pallas_call(
        functools.partial(_dsa_kernel, tq=tq, kc=kc, top_k=top_k, idx_bits=(seq - 1).bit_length()),
        grid=(batch, nq),
        in_specs=[qrow(A_HEADS * LANES), qrow(IDX_HEADS * LANES),
                  pl.BlockSpec((1, SUBLANES, tq), lambda b, i: ((b * nq + i) // r, 0, (b * nq + i) % r)),
                  pl.BlockSpec((seq, LANES), lambda b, i: (b, 0)),
                  pl.BlockSpec((seq // kc, A_HEAD_DIM, kc), lambda b, i: (b, 0, 0))],
        out_specs=qrow(A_HEADS * A_HEAD_DIM),
        out_shape=jax.ShapeDtypeStruct((T, A_HEADS * A_HEAD_DIM), BF16),
        scratch_shapes=[pltpu.VMEM((seq // kc, kc, tq), I32),
                        pltpu.VMEM((WORD, seq // WORD, tq), I32),
                        pltpu.VMEM((1, A_HEADS * tq), F32),
                        pltpu.VMEM((A_HEAD_DIM + ONES_ROWS, A_HEADS * tq), F32),
                        pltpu.VMEM((kc, A_HEADS * tq), F32),
                        pltpu.VMEM((kc, A_HEADS * tq), F32)],
        compiler_params=_cparams(("arbitrary", "arbitrary")),
        name="dsa_attention",
    )(qa, qi, wit, kk, vat)


MLA_HEADS_PER_STEP = 4


def _mla_kernel(q_ref, k_ref, vt_ref, o_ref, m_ref, acc_ref, sa_ref, sb_ref, *, tq, nh):
    i = pl.program_id(2)
    nq = vt_ref.shape[0]
    nt = (((1,), (1,)), ((), ()))
    k_pos = lax.broadcasted_iota(I32, (tq, tq), 0)
    q_pos = lax.broadcasted_iota(I32, (tq, tq), 1)
    bias_lane = lax.broadcasted_iota(I32, (1, LANES), 1) == BIAS_LANE
    ones = jnp.ones((ONES_ROWS, tq), BF16)
    m_ref[...] = jnp.full(m_ref.shape, NEG_BIG, F32)
    acc_ref[...] = jnp.zeros(acc_ref.shape, F32)

    def scores(c, hh, s_ref, valid):
        k0 = pl.multiple_of(jnp.minimum(c, nq - 1) * tq, tq)
        kblk = k_ref[pl.ds(k0, tq), hh * LANES:(hh + 1) * LANES]
        if valid is not None:
            off = jnp.where(jnp.logical_and(bias_lane, jnp.logical_not(valid)), NEG_BIG, 0.0)
            kblk = kblk + off.astype(BF16)
        s_ref[hh] = lax.dot_general(kblk, q_ref[:, hh * LANES:(hh + 1) * LANES], nt,
                                    preferred_element_type=F32)

    def softmax_pv(c, hh, s_ref, diagonal):
        vt_aug = jnp.concatenate([vt_ref[jnp.minimum(c, nq - 1), hh * B_V:(hh + 1) * B_V, :], ones], axis=0)
        s = s_ref[hh]
        if diagonal:
            s = jnp.where(k_pos <= q_pos, s, NEG_BIG)
        m_prev = m_ref[hh]
        m_new = jnp.maximum(m_prev, jnp.max(s, axis=0, keepdims=True))
        alpha = jnp.exp2(m_prev - m_new)
        p = jnp.exp2(s - m_new).astype(BF16)
        acc_ref[hh] = alpha * acc_ref[hh] + jnp.dot(vt_aug, p, preferred_element_type=F32)
        m_ref[hh] = m_new

    def half_trip(cur, cur_ref, nxt, nxt_ref, diagonal=False):
        for hh in range(nh):
            scores(nxt, hh, nxt_ref, nxt < i)
            softmax_pv(cur, hh, cur_ref, diagonal)

    for hh in range(nh):
        scores(i, hh, sa_ref, None)
    half_trip(i, sa_ref, 0, sb_ref, diagonal=True)

    def trip(t, carry):
        half_trip(2 * t, sb_ref, 2 * t + 1, sa_ref)
        half_trip(2 * t + 1, sa_ref, 2 * t + 2, sb_ref)
        return carry

    lax.fori_loop(0, (i + 1) // 2, trip, 0)
    outs = [acc_ref[hh, 0:B_V, :] / acc_ref[hh, B_V:B_V + 1, :] for hh in range(nh)]
    o_ref[...] = jnp.concatenate(outs, axis=0).T.astype(BF16)


def _mla_attention(q, k, vt, batch, seq, tq):
    T = q.shape[0]
    nq = seq // tq
    nh = MLA_HEADS_PER_STEP
    return pl.pallas_call(
        functools.partial(_mla_kernel, tq=tq, nh=nh),
        grid=(batch, B_HEADS // nh, nq),
        in_specs=[pl.BlockSpec((tq, nh * LANES), lambda b, h, i: (b * nq + i, h)),
                  pl.BlockSpec((seq, nh * LANES), lambda b, h, i: (b, h)),
                  pl.BlockSpec((nq, nh * B_V, tq), lambda b, h, i: (b, h, 0))],
        out_specs=pl.BlockSpec((tq, nh * B_V), lambda b, h, i: (b * nq + i, h)),
        out_shape=jax.ShapeDtypeStruct((T, B_HEADS * B_V), BF16),
        scratch_shapes=[pltpu.VMEM((nh, 1, tq), F32), pltpu.VMEM((nh, B_V + ONES_ROWS, tq), F32),
                        pltpu.VMEM((nh, tq, tq), F32), pltpu.VMEM((nh, tq, tq), F32)],
        compiler_params=_cparams(("parallel", "parallel", "arbitrary")),
        name="mla_attention",
    )(q, k, vt)


def _merge_kernel(x_ref, ya_ref, yb_ref, yc_ref, wg_ref, bg_ref, wbr_ref, wo_ref, g_ref, b_ref, o_ref,
                  *, alpha, d):
    x = x_ref[...]
    xb = x.astype(BF16)
    merged = None
    for k, y_ref in enumerate((ya_ref, yb_ref, yc_ref)):
        pre = jnp.dot(xb, wg_ref[:, k * d:(k + 1) * d], preferred_element_type=F32) + bg_ref[k:k + 1, :]
        br = jnp.dot(y_ref[...], wbr_ref[k], preferred_element_type=F32)
        term = _sigmoid(pre) * br
        merged = term if merged is None else merged + term
    mix = jnp.dot(merged.astype(BF16), wo_ref[...], preferred_element_type=F32)
    o_ref[...] = _layer_norm(alpha * x + mix, g_ref[...], b_ref[...])


def _merge(x2, ya, yb, yc, wg, bg, wbr, wo, g, b, alpha, tm):
    T, D = x2.shape
    row = lambda w: pl.BlockSpec((tm, w), lambda i: (i, 0))
    full = lambda a: pl.BlockSpec(a.shape, lambda i: (0,) * a.ndim, pipeline_mode=pl.Buffered(1))
    return pl.pallas_call(
        functools.partial(_merge_kernel, alpha=alpha, d=D),
        grid=(T // tm,),
        in_specs=[row(D), row(ya.shape[1]), row(yb.shape[1]), row(yc.shape[1]),
                  full(wg), full(bg), full(wbr), full(wo), full(g), full(b)],
        out_specs=row(D),
        out_shape=jax.ShapeDtypeStruct((T, D), F32),
        compiler_params=_cparams(("parallel",)),
        name="merge_ln1",
    )(x2, ya, yb, yc, wg, bg, wbr, wo, g, b)


def _ffn_kernel(x_ref, p_ref, wg_ref, wu_ref, wo_ref, wpg_ref, wpp_ref, g_ref, b_ref, o_ref, *, alpha, tf):
    x = x_ref[...]
    xb = x.astype(BF16)
    nchunk = wg_ref.shape[1] // tf

    def hidden(c):
        fg = jnp.dot(xb, wg_ref[:, c * tf:(c + 1) * tf], preferred_element_type=F32)
        fu = jnp.dot(xb, wu_ref[:, c * tf:(c + 1) * tf], preferred_element_type=F32)
        return (fg * _sigmoid(fg) * fu).astype(BF16)

    def down(a, c):
        return jnp.dot(a, wo_ref[c * tf:(c + 1) * tf, :], preferred_element_type=F32)

    gate = _sigmoid(jnp.dot(xb, wpg_ref[...], preferred_element_type=F32))
    emb = jnp.dot(p_ref[...].astype(BF16), wpp_ref[...], preferred_element_type=F32)
    acc = alpha * x + gate * emb
    a_prev = hidden(0)
    for c in range(1, nchunk):
        a_cur = hidden(c)
        acc = acc + down(a_prev, c - 1)
        a_prev = a_cur
    acc = acc + down(a_prev, nchunk - 1)
    o_ref[...] = _layer_norm(acc, g_ref[...], b_ref[...])


def _ffn(x2, p3, layer, wg, wu, wo, wpg, wpp, g, b, alpha, tm):
    T, D = x2.shape
    row = lambda w: pl.BlockSpec((tm, w), lambda i: (i, 0))
    full = lambda a: pl.BlockSpec(a.shape, lambda i: (0,) * a.ndim, pipeline_mode=pl.Buffered(1))
    return pl.pallas_call(
        functools.partial(_ffn_kernel, alpha=alpha, tf=_ffn_tile(wg.shape[1])),
        grid=(T // tm,),
        in_specs=[row(D), pl.BlockSpec((None, tm, p3.shape[2]), lambda i: (layer, i, 0)),
                  full(wg), full(wu), full(wo), full(wpg), full(wpp), full(g), full(b)],
        out_specs=row(D),
        out_shape=jax.ShapeDtypeStruct((T, D), F32),
        compiler_params=_cparams(("parallel",)),
        name="ffn_ple_ln2",
    )(x2, p3, wg, wu, wo, wpg, wpp, g, b)


def _rope_tables(positions, dim, lane_lo, period):
    half = dim // 2
    inv = 1.0 / (ROPE_THETA ** (jnp.arange(0, dim, 2, dtype=F32) / dim))
    ang = positions.reshape(-1).astype(F32)[:, None] * inv
    cos = jnp.tile(jnp.cos(ang), (1, LANES // half))
    sin = jnp.tile(jnp.sin(ang), (1, LANES // half))
    rel = jnp.arange(LANES) % period - lane_lo
    inside = (rel >= 0) & (rel < dim)
    lo = inside & (rel < half)
    hi = inside & (rel >= half)
    c = jnp.where(inside[None], cos, 1.0)
    sa = jnp.where(lo[None], -sin, 0.0)
    sb = jnp.where(hi[None], sin, 0.0)
    return c, sa, sb


def _pad_cols(w, n):
    return jnp.pad(w, ((0, 0), (0, n - w.shape[1])))


def _ffn_tile(ff):
    for tf in (512, 256, 128):
        if ff % tf == 0:
            return tf
    return ff


def _prep_layer(w_in, w_uq, w_ukv, w_ffn_in, layer):
    d = w_in.shape[1]
    sizes = (A_HEADS * A_HEAD_DIM, A_HEAD_DIM, A_HEAD_DIM, IDX_HEADS * IDX_DIM, IDX_DIM, IDX_HEADS,
             Q_LORA, KV_LORA, B_ROPE, 2 * C_CH, N_BRANCH * d)
    bounds = [sum(sizes[:k]) for k in range(len(sizes) + 1)]
    qa, ka, va, qi, ki, wi, cq, ckv, kr, conv, gate = (w_in[layer, :, bounds[k]:bounds[k + 1]]
                                                       for k in range(len(sizes)))
    w_uq, w_ukv = w_uq[layer], w_ukv[layer]
    bf = lambda a: a.astype(BF16)
    qa = jnp.pad(bf(qa * (A_HEAD_DIM ** -0.5 * LOG2E)).reshape(d, A_HEADS, A_HEAD_DIM),
                 ((0, 0), (0, 0), (0, LANES - A_HEAD_DIM))).reshape(d, A_HEADS * LANES)
    qi = jnp.pad(bf(qi).reshape(d, IDX_HEADS, IDX_DIM),
                 ((0, 0), (0, 0), (LANES - IDX_DIM, 0))).reshape(d, IDX_HEADS * LANES)
    wi = bf(wi * (IDX_DIM ** -0.5 * IDX_HEADS ** -0.5))
    w_a = jnp.concatenate([qa, qi, bf(ka), bf(ki), _pad_cols(bf(va), LANES), _pad_cols(wi, LANES)], axis=1)
    kr = jnp.pad(bf(kr), ((0, 0), (B_NOPE, LANES - B_NOPE - B_ROPE)))
    w_b = jnp.concatenate([bf(cq), bf(ckv), kr], axis=1)
    wuq = jnp.pad(bf(w_uq).reshape(-1, B_HEADS, B_NOPE + B_ROPE),
                  ((0, 0), (0, 0), (0, LANES - B_NOPE - B_ROPE))).reshape(-1, B_HEADS * LANES)
    ukv = bf(w_ukv).reshape(-1, B_HEADS, B_NOPE + B_V)
    wuk = jnp.pad(ukv[:, :, :B_NOPE], ((0, 0), (0, 0), (0, LANES - B_NOPE))).reshape(-1, B_HEADS * LANES)
    wuv = ukv[:, :, B_NOPE:].reshape(-1, B_HEADS * B_V)
    wukv = jnp.concatenate([wuk, wuv], axis=1)
    ff = w_ffn_in.shape[2] // 2
    return dict(w_a=w_a, w_b=w_b, w_c=bf(conv), w_g=bf(gate), wuq=wuq, wukv=wukv,
                wfg=bf(w_ffn_in[layer, :, :ff]), wfu=bf(w_ffn_in[layer, :, ff:]))


def kernel(x, p, positions, w_in, b_gate, q_norm_g, w_uq, kv_norm_g, w_ukv, conv_w, conv_b, conv_ln_g,
           conv_ln_b, w_branch, w_out, ln1_g, ln1_b, w_ffn_in, w_ffn_out, w_ple_gate, w_ple_proj, ln2_g, ln2_b):
    batch, seq, d = x.shape
    depth = w_in.shape[0]
    T = batch * seq
    alpha = float((2 * depth) ** 0.25)
    tm = min(512, seq)
    tq_a = min(256, seq)
    assert seq % tm == 0 and tm % tq_a == 0 and tq_a % LANES == 0 and tm % (WORD * SUBLANES) == 0, (seq, tm, tq_a)
    tabs_a = _rope_tables(positions, A_HEAD_DIM, 0, A_HEAD_DIM)
    tabs_b = _rope_tables(positions, B_ROPE, B_NOPE, LANES)
    row = lambda v: v.reshape(1, -1)
    x2 = x.reshape(T, d)
    for i in range(depth):
        w = _prep_layer(w_in, w_uq, w_ukv, w_ffn_in, i)
        qa, qi, kk, vat, wit = _proj_a(x2, w["w_a"], tabs_a, tm)
        qb, kb, vbt = _proj_b(x2, w["w_b"], row(q_norm_g[i]), row(kv_norm_g[i]), w["wuq"], w["wukv"], tabs_b, tm)
        y_c = _conv_branch(x2, w["w_c"], conv_w[i], row(conv_b[i]), row(conv_ln_g[i]), row(conv_ln_b[i]),
                           batch, seq, tm)
        y_a = _dsa_attention(qa, qi, wit, kk, vat, batch, seq, tq_a, tm)
        y_b = _mla_attention(qb, kb, vbt, batch, seq, tm)
        x2 = _merge(x2, y_a, y_b, y_c, w["w_g"], b_gate[i], w_branch[i].astype(BF16), w_out[i].astype(BF16),
                    row(ln1_g[i]), row(ln1_b[i]), alpha, tm)
        x2 = _ffn(x2, p.reshape(depth, T, -1), i, w["wfg"], w["wfu"], w_ffn_out[i].astype(BF16),
                  w_ple_gate[i].astype(BF16), w_ple_proj[i].astype(BF16), row(ln2_g[i]), row(ln2_b[i]),
                  alpha, tm)
    return x2.reshape(batch, seq, d)
```

```python
import functools

import jax
import jax.numpy as jnp
from jax import lax
from jax.experimental import pallas as pl
from jax.experimental.pallas import tpu as pltpu

F32 = jnp.float32
BF16 = jnp.bfloat16
I32 = jnp.int32

LANES = 128
SUBLANES = 8

PLE_DIM = 256
A_HEADS = 8
A_HEAD_DIM = 64
IDX_HEADS = 4
IDX_DIM = 64
TOPK_MAX = 256
B_HEADS = 8
B_NOPE = 64
B_ROPE = 32
B_V = 64
Q_LORA = 384
KV_LORA = 256
C_CH = 512
CONV_W = 31
N_BRANCH = 3
ROPE_THETA = 10000.0
LN_EPS = 1e-5
RMS_EPS = 1e-6

INT_MIN = -(2 ** 31)
NEG_BIG = -1e30
LOG2E = 1.4426950408889634

VMEM_LIMIT = 56 * 1024 * 1024


def _cparams(sem):
    return pltpu.CompilerParams(dimension_semantics=sem, vmem_limit_bytes=VMEM_LIMIT)


def _layer_norm(v, g, b):
    mu = jnp.mean(v, axis=-1, keepdims=True)
    d = v - mu
    var = jnp.mean(d * d, axis=-1, keepdims=True)
    return d * lax.rsqrt(var + LN_EPS) * g + b


def _sigmoid(v):
    return 1.0 / (1.0 + jnp.exp(-v))


def _tree(op, parts):
    while len(parts) > 1:
        parts = [op(a, b) for a, b in zip(parts[::2], parts[1::2])] + ([parts[-1]] if len(parts) % 2 else [])
    return parts[0]


def _reduce_rows(op, v):
    slab = _tree(op, [v[r:r + SUBLANES] for r in range(0, v.shape[0], SUBLANES)])
    return op.reduce(slab, axis=0, keepdims=True)


def _rope_group(h, c, sa, sb, half):
    return h * c + pltpu.roll(h, LANES - half, 1) * sa + pltpu.roll(h, half, 1) * sb


def _proj_a_kernel(x_ref, w_ref, c_ref, sa_ref, sb_ref, qa_ref, qi_ref, kk_ref, va_ref, wi_ref):
    xb = x_ref[...].astype(BF16)
    h = jnp.dot(xb, w_ref[...], preferred_element_type=F32)
    c, sa, sb = c_ref[...], sa_ref[...], sb_ref[...]
    half = A_HEAD_DIM // 2
    for g in range(A_HEADS):
        hg = h[:, g * LANES:(g + 1) * LANES]
        qa_ref[:, g * LANES:(g + 1) * LANES] = _rope_group(hg, c, sa, sb, half).astype(BF16)
    off = A_HEADS
    for g in range(IDX_HEADS):
        hg = h[:, (off + g) * LANES:(off + g + 1) * LANES]
        qi_ref[:, g * LANES:(g + 1) * LANES] = _rope_group(hg, c, sa, sb, half).astype(BF16)
    off += IDX_HEADS
    hg = h[:, off * LANES:(off + 1) * LANES]
    kk_ref[...] = _rope_group(hg, c, sa, sb, half).astype(BF16)
    off += 1
    va_ref[0] = h[:, off * LANES:(off + 1) * LANES].T[:A_HEAD_DIM, :].astype(BF16)
    off += 1
    wi_ref[0] = h[:, off * LANES:(off + 1) * LANES].T[:SUBLANES, :]


def _proj_a(x2, w_a, tabs, tm):
    T, D = x2.shape
    n = w_a.shape[1]
    row = lambda w: pl.BlockSpec((tm, w), lambda i: (i, 0))
    return pl.pallas_call(
        _proj_a_kernel,
        grid=(T // tm,),
        in_specs=[row(D), pl.BlockSpec((D, n), lambda i: (0, 0)), row(LANES), row(LANES), row(LANES)],
        out_specs=[row(A_HEADS * LANES), row(IDX_HEADS * LANES), row(LANES),
                   pl.BlockSpec((1, A_HEAD_DIM, tm), lambda i: (i, 0, 0)),
                   pl.BlockSpec((1, SUBLANES, tm), lambda i: (i, 0, 0))],
        out_shape=[jax.ShapeDtypeStruct((T, A_HEADS * LANES), BF16),
                   jax.ShapeDtypeStruct((T, IDX_HEADS * LANES), BF16),
                   jax.ShapeDtypeStruct((T, LANES), BF16),
                   jax.ShapeDtypeStruct((T // tm, A_HEAD_DIM, tm), BF16),
                   jax.ShapeDtypeStruct((T // tm, SUBLANES, tm), F32)],
        compiler_params=_cparams(("parallel",)),
        name="proj_a",
    )(x2, w_a, *tabs)


def _proj_b_kernel(x_ref, w_ref, gq_ref, gkv_ref, wuq_ref, wukv_ref, c_ref, sa_ref, sb_ref,
                   q_ref, k_ref, v_ref, *, q_scale):
    xb = x_ref[...].astype(BF16)
    h = jnp.dot(xb, w_ref[...], preferred_element_type=F32)
    c, sa, sb = c_ref[...], sa_ref[...], sb_ref[...]
    half = B_ROPE // 2
    cq = h[:, :Q_LORA]
    cqn = cq * lax.rsqrt(jnp.mean(cq * cq, axis=-1, keepdims=True) + RMS_EPS) * gq_ref[...]
    ckv = h[:, Q_LORA:Q_LORA + KV_LORA]
    ckvn = ckv * lax.rsqrt(jnp.mean(ckv * ckv, axis=-1, keepdims=True) + RMS_EPS) * gkv_ref[...]
    kr = _rope_group(h[:, Q_LORA + KV_LORA:], c, sa, sb, half)
    q = jnp.dot(cqn.astype(BF16), wuq_ref[...], preferred_element_type=F32)
    kv = jnp.dot(ckvn.astype(BF16), wukv_ref[...], preferred_element_type=F32)
    for g in range(B_HEADS):
        qg = q[:, g * LANES:(g + 1) * LANES]
        q_ref[:, g * LANES:(g + 1) * LANES] = (_rope_group(qg, c, sa, sb, half) * q_scale).astype(BF16)
        k_ref[:, g * LANES:(g + 1) * LANES] = (kv[:, g * LANES:(g + 1) * LANES] + kr).astype(BF16)
    v_ref[0] = kv[:, B_HEADS * LANES:].T.astype(BF16)


def _proj_b(x2, w_b, gq, gkv, wuq, wukv, tabs, tm):
    T, D = x2.shape
    row = lambda w: pl.BlockSpec((tm, w), lambda i: (i, 0))
    full = lambda a: pl.BlockSpec(a.shape, lambda i: (0, 0))
    q_scale = float((B_NOPE + B_ROPE) ** -0.5 * LOG2E)
    return pl.pallas_call(
        functools.partial(_proj_b_kernel, q_scale=q_scale),
        grid=(T // tm,),
        in_specs=[row(D), full(w_b), full(gq), full(gkv), full(wuq), full(wukv),
                  row(LANES), row(LANES), row(LANES)],
        out_specs=[row(B_HEADS * LANES), row(B_HEADS * LANES),
                   pl.BlockSpec((1, B_HEADS * B_V, tm), lambda i: (i, 0, 0))],
        out_shape=[jax.ShapeDtypeStruct((T, B_HEADS * LANES), BF16),
                   jax.ShapeDtypeStruct((T, B_HEADS * LANES), BF16),
                   jax.ShapeDtypeStruct((T // tm, B_HEADS * B_V, tm), BF16)],
        compiler_params=_cparams(("parallel",)),
        name="proj_b",
    )(x2, w_b, gq, gkv, wuq, wukv, *tabs)


CONV_HALO = 32
CONV_ROWS = 64


def _conv_kernel(x_ref, w_ref, cw_ref, cb_ref, g_ref, b_ref, o_ref, hbuf, *, tm):
    j = pl.program_id(1)
    span = CONV_HALO + tm

    @pl.when(j == 0)
    def _():
        hbuf[0, 0:CONV_HALO, :] = jnp.zeros((CONV_HALO, C_CH), F32)

    @pl.when(j > 0)
    def _():
        hbuf[0, 0:CONV_HALO, :] = hbuf[0, tm:span, :]

    xb = x_ref[...].astype(BF16)
    u = jnp.dot(xb, w_ref[...], preferred_element_type=F32)
    hbuf[0, CONV_HALO:span, :] = u[:, :C_CH] * _sigmoid(u[:, C_CH:])
    for r in range(1, SUBLANES):
        hbuf[r, SUBLANES:span, :] = hbuf[0, SUBLANES - r:span - r, :]
    for sb in range(tm // CONV_ROWS):
        r0 = sb * CONV_ROWS
        acc = jnp.zeros((CONV_ROWS, C_CH), F32) + cb_ref[...]
        for t in range(CONV_W):
            back = CONV_W - 1 - t
            r = back % SUBLANES
            s0 = CONV_HALO + r0 - back + r
            acc = acc + hbuf[r, s0:s0 + CONV_ROWS, :] * cw_ref[t:t + 1, :]
        y = _layer_norm(acc, g_ref[...], b_ref[...])
        o_ref[r0:r0 + CONV_ROWS, :] = (y * _sigmoid(y)).astype(BF16)


def _conv_branch(x2, w_c, cw, cb, g, b, batch, seq, tm):
    T, D = x2.shape
    nj = seq // tm
    row = lambda w: pl.BlockSpec((tm, w), lambda bi, j: (bi * nj + j, 0))
    full = lambda a: pl.BlockSpec(a.shape, lambda bi, j: (0, 0))
    return pl.pallas_call(
        functools.partial(_conv_kernel, tm=tm),
        grid=(batch, nj),
        in_specs=[row(D), full(w_c), full(cw), full(cb), full(g), full(b)],
        out_specs=row(C_CH),
        out_shape=jax.ShapeDtypeStruct((T, C_CH), BF16),
        scratch_shapes=[pltpu.VMEM((SUBLANES, tm + CONV_HALO, C_CH), F32)],
        compiler_params=_cparams(("arbitrary", "arbitrary")),
        name="conv_branch",
    )(x2, w_c, cw, cb, g, b)


WORD = 32
ONES_ROWS = 16


def _bit_planes(words):
    a = list(words)
    masks = {16: 0x0000FFFF, 8: 0x00FF00FF, 4: 0x0F0F0F0F, 2: 0x33333333, 1: 0x55555555}

    def swap(k, j):
        t = (a[k] ^ lax.shift_right_logical(a[k + j], jnp.int32(j))) & jnp.int32(masks[j])
        a[k] = a[k] ^ t
        a[k + j] = a[k + j] ^ lax.shift_left(t, jnp.int32(j))

    for base in range(0, WORD, 8):
        for j in (4, 2, 1):
            for k in range(base, base + 8):
                if not k & j:
                    swap(k, j)
    for r in range(8):
        for j in (16, 8):
            for k in range(r, WORD, 8):
                if not k & j:
                    swap(k, j)
    return a


def _dsa_kernel(qa_ref, qi_ref, wi_ref, kk_ref, vt_ref, o_ref, keys_ref, planes_ref, m_ref, acc_ref, sa_ref, sb_ref,
                *, tq, kc, top_k, idx_bits):
    i = pl.program_id(1)
    nkc = ((i + 1) * tq + kc - 1) // kc
    int_min = jnp.int32(INT_MIN)
    kthf = jnp.float32(top_k)
    nt = (((1,), (1,)), ((), ()))
    wrows = kc // WORD
    nwords = planes_ref.shape[1]

    qi = qi_ref[...]
    qi_st = jnp.concatenate([qi[:, h * LANES:(h + 1) * LANES] for h in range(IDX_HEADS)], axis=0)
    w = wi_ref[0]
    q_pos = i * tq + lax.broadcasted_iota(I32, (kc, tq), 1)
    k_iota = lax.broadcasted_iota(I32, (kc, tq), 0)

    @pl.when(i == 0)
    def _():
        planes_ref[...] = jnp.zeros(planes_ref.shape, I32)

    def score_chunk(c, diagonal):
        k0 = pl.multiple_of(c * kc, kc)
        kblk = kk_ref[pl.ds(k0, kc), :]
        lg = lax.dot_general(kblk, qi_st, nt, preferred_element_type=F32)
        sc = w[0:1, :] * jnp.maximum(lg[:, 0:tq], 0.0)
        for h in range(1, IDX_HEADS):
            sc = sc + w[h:h + 1, :] * jnp.maximum(lg[:, h * tq:(h + 1) * tq], 0.0)
        bits = lax.bitcast_convert_type(sc, I32)
        key = jnp.where(bits < 0, int_min - bits, bits)
        if diagonal:
            key = jnp.where(k0 + k_iota <= q_pos, key, int_min)
        keys_ref[c] = key
        w0 = pl.multiple_of(c * wrows, wrows)
        for g in range(wrows // SUBLANES):
            base = g * WORD * SUBLANES
            planes = _bit_planes([key[base + SUBLANES * j:base + SUBLANES * (j + 1)] for j in range(WORD)])
            planes[0] = ~planes[0]
            for s in range(WORD):
                planes_ref[s, pl.ds(w0 + SUBLANES * g, SUBLANES), :] = planes[s]

    def score_body(c, carry):
        score_chunk(c, False)
        return carry

    lax.fori_loop(0, nkc - 1, score_body, 0)
    score_chunk(nkc - 1, True)

    word_row = lax.broadcasted_iota(I32, (nwords, tq), 0)
    act0 = jnp.where(word_row < nkc * wrows, jnp.int32(-1), jnp.int32(0))

    def popcount_rows(v):
        cnt = _tree(jnp.add, [lax.population_count(v[r:r + SUBLANES]) for r in range(0, nwords, SUBLANES)])
        return jnp.sum(cnt.astype(F32), axis=0, keepdims=True)

    def bit_body(s, carry):
        act, rem, thr_u = carry
        cand = act & planes_ref[s]
        c1 = popcount_rows(cand)
        take = c1 >= rem
        act = jnp.where(take, cand, act ^ cand)
        rem = jnp.where(take, rem, rem - c1)
        thr_u = thr_u | jnp.where(take, lax.shift_left(jnp.int32(1), jnp.int32(31) - s), jnp.int32(0))
        return act, rem, thr_u

    act, rem, thr_u = lax.fori_loop(0, WORD, bit_body,
                                    (act0, jnp.full((1, tq), kthf, F32), jnp.zeros((1, tq), I32)))
    thr = thr_u ^ int_min

    need = jnp.logical_and(popcount_rows(act) > rem, thr > int_min)
    any_need = jnp.max(jnp.where(need, 1.0, 0.0)) > 0.0

    def last_tied_index():
        sub_bits = SUBLANES.bit_length() - 1
        group_bits = (WORD * SUBLANES).bit_length() - 1
        row_index = (word_row & (SUBLANES - 1)) | lax.shift_left(lax.shift_right_logical(word_row, sub_bits),
                                                                 group_bits)
        tied, left, j0 = act, rem, jnp.zeros((1, tq), I32)
        for b in reversed(range(idx_bits)):
            if sub_bits <= b < group_bits:
                pattern = sum(1 << j for j in range(WORD) if ((WORD - 1 - j) >> (b - sub_bits)) & 1)
                plane = jnp.int32(pattern - 2 ** 32 if pattern >= 2 ** 31 else pattern)
            else:
                plane = -(lax.shift_right_logical(row_index, b) & 1)
            low = tied & ~plane
            n_low = popcount_rows(low)
            stay = n_low >= left
            tied = jnp.where(stay, low, tied & plane)
            left = jnp.where(stay, left, left - n_low)
            j0 = j0 | jnp.where(stay, jnp.int32(0), jnp.int32(1 << b))
        return j0

    j0 = lax.cond(any_need, last_tied_index, lambda: jnp.zeros((1, tq), I32))
    j0 = jnp.where(need, j0, jnp.where(thr > int_min, jnp.int32(2 ** 31 - 1), jnp.int32(-1)))

    q = qa_ref[...]
    sub = LANES
    eye = jnp.where(lax.broadcasted_iota(I32, (sub, sub), 0) == lax.broadcasted_iota(I32, (sub, sub), 1),
                    1.0, 0.0).astype(BF16)
    npairs = A_HEADS // 2
    units = [(u, hp) for u in range(tq // sub) for hp in range(npairs)]
    q_unit = [jnp.concatenate([jnp.concatenate([q[u * sub:(u + 1) * sub, h * LANES:(h + 1) * LANES], eye], axis=1)
                               for h in (2 * hp, 2 * hp + 1)], axis=0) for u, hp in units]
    cols = [slice(n * 2 * sub, (n + 1) * 2 * sub) for n in range(len(units))]
    ones = jnp.ones((ONES_ROWS, kc), BF16)
    m_ref[...] = jnp.full(m_ref.shape, NEG_BIG, F32)
    acc_ref[...] = jnp.zeros(acc_ref.shape, F32)

    def keys_aug(c):
        cc = jnp.minimum(c, nkc - 1)
        k0 = pl.multiple_of(cc * kc, kc)
        kb = keys_ref[cc]
        tie_bias = jnp.where(k0 + k_iota <= j0, 0.0, NEG_BIG)
        bias = jnp.where(kb > thr, 0.0, jnp.where(kb == thr, tie_bias, NEG_BIG)).astype(BF16)
        kblk = kk_ref[pl.ds(k0, kc), :]
        return [jnp.concatenate([kblk, bias[:, u * sub:(u + 1) * sub]], axis=1)
                for u in range(tq // sub)]

    def scores(k_aug, n, s_ref):
        s_ref[:, cols[n]] = lax.dot_general(k_aug[units[n][0]], q_unit[n], nt,
                                            preferred_element_type=F32)

    def softmax(n, s_ref):
        s = s_ref[:, cols[n]]
        m_prev = m_ref[:, cols[n]]
        m_new = jnp.maximum(m_prev, jnp.max(s, axis=0, keepdims=True))
        m_ref[:, cols[n]] = m_new
        return jnp.exp2(m_prev - m_new), jnp.exp2(s - m_new).astype(BF16)

    def weighted_values(vt_aug, n, alpha, p):
        acc_ref[:, cols[n]] = alpha * acc_ref[:, cols[n]] + jnp.dot(vt_aug, p, preferred_element_type=F32)

    def half_trip(c, cur_ref, nxt_ref):
        k_aug = keys_aug(c + 1)
        vt_aug = jnp.concatenate([vt_ref[c], ones], axis=0)
        pending = None
        for n in range(len(units)):
            scores(k_aug, n, nxt_ref)
            if pending is not None:
                weighted_values(vt_aug, n - 1, *pending)
            pending = softmax(n, cur_ref)
        weighted_values(vt_aug, len(units) - 1, *pending)

    def att_body(j, carry):
        half_trip(2 * j, sa_ref, sb_ref)
        half_trip(2 * j + 1, sb_ref, sa_ref)
        return carry

    k_aug0 = keys_aug(0)
    for n in range(len(units)):
        scores(k_aug0, n, sa_ref)
    lax.fori_loop(0, nkc // 2, att_body, 0)

    @pl.when(nkc % 2 == 1)
    def _():
        vt_aug = jnp.concatenate([vt_ref[nkc - 1], ones], axis=0)
        for n in range(len(units)):
            weighted_values(vt_aug, n, *softmax(n, sa_ref))

    out = acc_ref[0:A_HEAD_DIM, :] / acc_ref[A_HEAD_DIM:A_HEAD_DIM + 1, :]
    for n, (u, hp) in enumerate(units):
        o2 = out[:, cols[n]]
        pair = jnp.concatenate([o2[:, :sub], o2[:, sub:]], axis=0)
        o_ref[u * sub:(u + 1) * sub, hp * LANES:(hp + 1) * LANES] = pair.T.astype(BF16)


def _dsa_attention(qa, qi, wit, kk, vat, batch, seq, tq, kc):
    T = qa.shape[0]
    nq = seq // tq
    r = kc // tq
    top_k = min(TOPK_MAX, seq // 4)
    qrow = lambda w: pl.BlockSpec((tq, w), lambda b, i: (b * nq + i, 0))
    return pl.pallas_call(
        functools.partial(_dsa_kernel, tq=tq, kc=kc, top_k=top_k, idx_bits=(seq - 1).bit_length()),
        grid=(batch, nq),
        in_specs=[qrow(A_HEADS * LANES), qrow(IDX_HEADS * LANES),
                  pl.BlockSpec((1, SUBLANES, tq), lambda b, i: ((b * nq + i) // r, 0, (b * nq + i) % r)),
                  pl.BlockSpec((seq, LANES), lambda b, i: (b, 0)),
                  pl.BlockSpec((seq // kc, A_HEAD_DIM, kc), lambda b, i: (b, 0, 0))],
        out_specs=qrow(A_HEADS * A_HEAD_DIM),
        out_shape=jax.ShapeDtypeStruct((T, A_HEADS * A_HEAD_DIM), BF16),
        scratch_shapes=[pltpu.VMEM((seq // kc, kc, tq), I32),
                        pltpu.VMEM((WORD, seq // WORD, tq), I32),
                        pltpu.VMEM((1, A_HEADS * tq), F32),
                        pltpu.VMEM((A_HEAD_DIM + ONES_ROWS, A_HEADS * tq), F32),
                        pltpu.VMEM((kc, A_HEADS * tq), F32),
                        pltpu.VMEM((kc, A_HEADS * tq), F32)],
        compiler_params=_cparams(("arbitrary", "arbitrary")),
        name="dsa_attention",
    )(qa, qi, wit, kk, vat)


MLA_HEADS_PER_STEP = 4


def _mla_kernel(q_ref, k_ref, vt_ref, o_ref, m_ref, acc_ref, sa_ref, sb_ref, *, tq, nh):
    i = pl.program_id(2)
    nt = (((1,), (1,)), ((), ()))
    k_pos = lax.broadcasted_iota(I32, (tq, tq), 0)
    q_pos = lax.broadcasted_iota(I32, (tq, tq), 1)
    ones = jnp.ones((ONES_ROWS, tq), BF16)
    m_ref[...] = jnp.full(m_ref.shape, NEG_BIG, F32)
    acc_ref[...] = jnp.zeros(acc_ref.shape, F32)

    def scores(c, hh, s_ref):
        k0 = pl.multiple_of(c * tq, tq)
        s_ref[hh] = lax.dot_general(k_ref[pl.ds(k0, tq), hh * LANES:(hh + 1) * LANES],
                                    q_ref[:, hh * LANES:(hh + 1) * LANES], nt,
                                    preferred_element_type=F32)

    def softmax_pv(c, hh, s_ref, diagonal):
        vt_aug = jnp.concatenate([vt_ref[c, hh * B_V:(hh + 1) * B_V, :], ones], axis=0)
        s = s_ref[hh]
        if diagonal:
            s = jnp.where(k_pos <= q_pos, s, NEG_BIG)
        m_prev = m_ref[hh]
        m_new = jnp.maximum(m_prev, jnp.max(s, axis=0, keepdims=True))
        alpha = jnp.exp2(m_prev - m_new)
        p = jnp.exp2(s - m_new).astype(BF16)
        acc_ref[hh] = alpha * acc_ref[hh] + jnp.dot(vt_aug, p, preferred_element_type=F32)
        m_ref[hh] = m_new

    def half_trip(cur, cur_ref, nxt, nxt_ref, diagonal=False):
        for hh in range(nh):
            scores(nxt, hh, nxt_ref)
            softmax_pv(cur, hh, cur_ref, diagonal)

    for hh in range(nh):
        scores(i, hh, sa_ref)
    half_trip(i, sa_ref, 0, sb_ref, diagonal=True)

    def trip(t, carry):
        half_trip(2 * t, sb_ref, 2 * t + 1, sa_ref)
        half_trip(2 * t + 1, sa_ref, 2 * t + 2, sb_ref)
        return carry

    lax.fori_loop(0, i // 2, trip, 0)

    @pl.when(i % 2 == 1)
    def _():
        for hh in range(nh):
            softmax_pv(i - 1, hh, sb_ref, False)

    outs = [acc_ref[hh, 0:B_V, :] / acc_ref[hh, B_V:B_V + 1, :] for hh in range(nh)]
    o_ref[...] = jnp.concatenate(outs, axis=0).T.astype(BF16)


def _mla_attention(q, k, vt, batch, seq, tq):
    T = q.shape[0]
    nq = seq // tq
    nh = MLA_HEADS_PER_STEP
    return pl.pallas_call(
        functools.partial(_mla_kernel, tq=tq, nh=nh),
        grid=(batch, B_HEADS // nh, nq),
        in_specs=[pl.BlockSpec((tq, nh * LANES), lambda b, h, i: (b * nq + i, h)),
                  pl.BlockSpec((seq, nh * LANES), lambda b, h, i: (b, h)),
                  pl.BlockSpec((nq, nh * B_V, tq), lambda b, h, i: (b, h, 0))],
        out_specs=pl.BlockSpec((tq, nh * B_V), lambda b, h, i: (b * nq + i, h)),
        out_shape=jax.ShapeDtypeStruct((T, B_HEADS * B_V), BF16),
        scratch_shapes=[pltpu.VMEM((nh, 1, tq), F32), pltpu.VMEM((nh, B_V + ONES_ROWS, tq), F32),
                        pltpu.VMEM((nh, tq, tq), F32), pltpu.VMEM((nh, tq, tq), F32)],
        compiler_params=_cparams(("parallel", "parallel", "arbitrary")),
        name="mla_attention",
    )(q, k, vt)


def _merge_kernel(x_ref, ya_ref, yb_ref, yc_ref, wg_ref, bg_ref, wbr_ref, wo_ref, g_ref, b_ref, o_ref,
                  *, alpha, d):
    x = x_ref[...]
    xb = x.astype(BF16)
    merged = None
    for k, y_ref in enumerate((ya_ref, yb_ref, yc_ref)):
        pre = jnp.dot(xb, wg_ref[:, k * d:(k + 1) * d], preferred_element_type=F32) + bg_ref[k:k + 1, :]
        br = jnp.dot(y_ref[...], wbr_ref[k], preferred_element_type=F32)
        term = _sigmoid(pre) * br
        merged = term if merged is None else merged + term
    mix = jnp.dot(merged.astype(BF16), wo_ref[...], preferred_element_type=F32)
    o_ref[...] = _layer_norm(alpha * x + mix, g_ref[...], b_ref[...])


def _merge(x2, ya, yb, yc, wg, bg, wbr, wo, g, b, alpha, tm):
    T, D = x2.shape
    row = lambda w: pl.BlockSpec((tm, w), lambda i: (i, 0))
    full = lambda a: pl.BlockSpec(a.shape, lambda i: (0,) * a.ndim, pipeline_mode=pl.Buffered(1))
    return pl.pallas_call(
        functools.partial(_merge_kernel, alpha=alpha, d=D),
        grid=(T // tm,),
        in_specs=[row(D), row(ya.shape[1]), row(yb.shape[1]), row(yc.shape[1]),
                  full(wg), full(bg), full(wbr), full(wo), full(g), full(b)],
        out_specs=row(D),
        out_shape=jax.ShapeDtypeStruct((T, D), F32),
        compiler_params=_cparams(("parallel",)),
        name="merge_ln1",
    )(x2, ya, yb, yc, wg, bg, wbr, wo, g, b)


def _ffn_kernel(x_ref, p_ref, wg_ref, wu_ref, wo_ref, wpg_ref, wpp_ref, g_ref, b_ref, o_ref, *, alpha, tf):
    x = x_ref[...]
    xb = x.astype(BF16)
    nchunk = wg_ref.shape[1] // tf

    def hidden(c):
        fg = jnp.dot(xb, wg_ref[:, c * tf:(c + 1) * tf], preferred_element_type=F32)
        fu = jnp.dot(xb, wu_ref[:, c * tf:(c + 1) * tf], preferred_element_type=F32)
        return (fg * _sigmoid(fg) * fu).astype(BF16)

    def down(a, c):
        return jnp.dot(a, wo_ref[c * tf:(c + 1) * tf, :], preferred_element_type=F32)

    gate = _sigmoid(jnp.dot(xb, wpg_ref[...], preferred_element_type=F32))
    emb = jnp.dot(p_ref[...].astype(BF16), wpp_ref[...], preferred_element_type=F32)
    acc = alpha * x + gate * emb
    a_prev = hidden(0)
    for c in range(1, nchunk):
        a_cur = hidden(c)
        acc = acc + down(a_prev, c - 1)
        a_prev = a_cur
    acc = acc + down(a_prev, nchunk - 1)
    o_ref[...] = _layer_norm(acc, g_ref[...], b_ref[...])


def _ffn(x2, p3, layer, wg, wu, wo, wpg, wpp, g, b, alpha, tm):
    T, D = x2.shape
    row = lambda w: pl.BlockSpec((tm, w), lambda i: (i, 0))
    full = lambda a: pl.BlockSpec(a.shape, lambda i: (0,) * a.ndim, pipeline_mode=pl.Buffered(1))
    return pl.pallas_call(
        functools.partial(_ffn_kernel, alpha=alpha, tf=_ffn_tile(wg.shape[1])),
        grid=(T // tm,),
        in_specs=[row(D), pl.BlockSpec((None, tm, p3.shape[2]), lambda i: (layer, i, 0)),
                  full(wg), full(wu), full(wo), full(wpg), full(wpp), full(g), full(b)],
        out_specs=row(D),
        out_shape=jax.ShapeDtypeStruct((T, D), F32),
        compiler_params=_cparams(("parallel",)),
        name="ffn_ple_ln2",
    )(x2, p3, wg, wu, wo, wpg, wpp, g, b)


def _rope_tables(positions, dim, lane_lo, period):
    half = dim // 2
    inv = 1.0 / (ROPE_THETA ** (jnp.arange(0, dim, 2, dtype=F32) / dim))
    ang = positions.reshape(-1).astype(F32)[:, None] * inv
    cos = jnp.tile(jnp.cos(ang), (1, LANES // half))
    sin = jnp.tile(jnp.sin(ang), (1, LANES // half))
    rel = jnp.arange(LANES) % period - lane_lo
    inside = (rel >= 0) & (rel < dim)
    lo = inside & (rel < half)
    hi = inside & (rel >= half)
    c = jnp.where(inside[None], cos, 1.0)
    sa = jnp.where(lo[None], -sin, 0.0)
    sb = jnp.where(hi[None], sin, 0.0)
    return c, sa, sb


def _pad_cols(w, n):
    return jnp.pad(w, ((0, 0), (0, n - w.shape[1])))


def _ffn_tile(ff):
    for tf in (512, 256, 128):
        if ff % tf == 0:
            return tf
    return ff


def _prep_layer(w_in, w_uq, w_ukv, w_ffn_in, layer):
    d = w_in.shape[1]
    sizes = (A_HEADS * A_HEAD_DIM, A_HEAD_DIM, A_HEAD_DIM, IDX_HEADS * IDX_DIM, IDX_DIM, IDX_HEADS,
             Q_LORA, KV_LORA, B_ROPE, 2 * C_CH, N_BRANCH * d)
    bounds = [sum(sizes[:k]) for k in range(len(sizes) + 1)]
    qa, ka, va, qi, ki, wi, cq, ckv, kr, conv, gate = (w_in[layer, :, bounds[k]:bounds[k + 1]]
                                                       for k in range(len(sizes)))
    w_uq, w_ukv = w_uq[layer], w_ukv[layer]
    bf = lambda a: a.astype(BF16)
    qa = jnp.pad(bf(qa * (A_HEAD_DIM ** -0.5 * LOG2E)).reshape(d, A_HEADS, A_HEAD_DIM),
                 ((0, 0), (0, 0), (0, LANES - A_HEAD_DIM))).reshape(d, A_HEADS * LANES)
    qi = jnp.pad(bf(qi).reshape(d, IDX_HEADS, IDX_DIM),
                 ((0, 0), (0, 0), (LANES - IDX_DIM, 0))).reshape(d, IDX_HEADS * LANES)
    wi = bf(wi * (IDX_DIM ** -0.5 * IDX_HEADS ** -0.5))
    w_a = jnp.concatenate([qa, qi, bf(ka), bf(ki), _pad_cols(bf(va), LANES), _pad_cols(wi, LANES)], axis=1)
    kr = jnp.pad(bf(kr), ((0, 0), (B_NOPE, LANES - B_NOPE - B_ROPE)))
    w_b = jnp.concatenate([bf(cq), bf(ckv), kr], axis=1)
    wuq = jnp.pad(bf(w_uq).reshape(-1, B_HEADS, B_NOPE + B_ROPE),
                  ((0, 0), (0, 0), (0, LANES - B_NOPE - B_ROPE))).reshape(-1, B_HEADS * LANES)
    ukv = bf(w_ukv).reshape(-1, B_HEADS, B_NOPE + B_V)
    wuk = jnp.pad(ukv[:, :, :B_NOPE], ((0, 0), (0, 0), (0, LANES - B_NOPE))).reshape(-1, B_HEADS * LANES)
    wuv = ukv[:, :, B_NOPE:].reshape(-1, B_HEADS * B_V)
    wukv = jnp.concatenate([wuk, wuv], axis=1)
    ff = w_ffn_in.shape[2] // 2
    return dict(w_a=w_a, w_b=w_b, w_c=bf(conv), w_g=bf(gate), wuq=wuq, wukv=wukv,
                wfg=bf(w_ffn_in[layer, :, :ff]), wfu=bf(w_ffn_in[layer, :, ff:]))


def kernel(x, p, positions, w_in, b_gate, q_norm_g, w_uq, kv_norm_g, w_ukv, conv_w, conv_b, conv_ln_g,
           conv_ln_b, w_branch, w_out, ln1_g, ln1_b, w_ffn_in, w_ffn_out, w_ple_gate, w_ple_proj, ln2_g, ln2_b):
    batch, seq, d = x.shape
    depth = w_in.shape[0]
    T = batch * seq
    alpha = float((2 * depth) ** 0.25)
    tm = min(512, seq)
    tq_a = min(256, seq)
    assert seq % tm == 0 and tm % tq_a == 0 and tq_a % LANES == 0 and tm % (WORD * SUBLANES) == 0, (seq, tm, tq_a)
    tabs_a = _rope_tables(positions, A_HEAD_DIM, 0, A_HEAD_DIM)
    tabs_b = _rope_tables(positions, B_ROPE, B_NOPE, LANES)
    row = lambda v: v.reshape(1, -1)
    x2 = x.reshape(T, d)
    for i in range(depth):
        w = _prep_layer(w_in, w_uq, w_ukv, w_ffn_in, i)
        qa, qi, kk, vat, wit = _proj_a(x2, w["w_a"], tabs_a, tm)
        qb, kb, vbt = _proj_b(x2, w["w_b"], row(q_norm_g[i]), row(kv_norm_g[i]), w["wuq"], w["wukv"], tabs_b, tm)
        y_c = _conv_branch(x2, w["w_c"], conv_w[i], row(conv_b[i]), row(conv_ln_g[i]), row(conv_ln_b[i]),
                           batch, seq, tm)
        y_a = _dsa_attention(qa, qi, wit, kk, vat, batch, seq, tq_a, tm)
        y_b = _mla_attention(qb, kb, vbt, batch, seq, tm)
        x2 = _merge(x2, y_a, y_b, y_c, w["w_g"], b_gate[i], w_branch[i].astype(BF16), w_out[i].astype(BF16),
                    row(ln1_g[i]), row(ln1_b[i]), alpha, tm)
        x2 = _ffn(x2, p.reshape(depth, T, -1), i, w["wfg"], w["wfu"], w_ffn_out[i].astype(BF16),
                  w_ple_gate[i].astype(BF16), w_ple_proj[i].astype(BF16), row(ln2_g[i]), row(ln2_b[i]),
                  alpha, tm)
    return x2.reshape(batch, seq, d)
```

```python
import functools

import jax
import jax.numpy as jnp
from jax import lax
from jax.experimental import pallas as pl
from jax.experimental.pallas import tpu as pltpu

F32 = jnp.float32
BF16 = jnp.bfloat16
I32 = jnp.int32

LANES = 128
SUBLANES = 8

PLE_DIM = 256
A_HEADS = 8
A_HEAD_DIM = 64
IDX_HEADS = 4
IDX_DIM = 64
TOPK_MAX = 256
B_HEADS = 8
B_NOPE = 64
B_ROPE = 32
B_V = 64
Q_LORA = 384
KV_LORA = 256
C_CH = 512
CONV_W = 31
N_BRANCH = 3
ROPE_THETA = 10000.0
LN_EPS = 1e-5
RMS_EPS = 1e-6

INT_MIN = -(2 ** 31)
NEG_BIG = -1e30
LOG2E = 1.4426950408889634

VMEM_LIMIT = 56 * 1024 * 1024


def _cparams(sem):
    return pltpu.CompilerParams(dimension_semantics=sem, vmem_limit_bytes=VMEM_LIMIT)


def _layer_norm(v, g, b):
    mu = jnp.mean(v, axis=-1, keepdims=True)
    d = v - mu
    var = jnp.mean(d * d, axis=-1, keepdims=True)
    return d * lax.rsqrt(var + LN_EPS) * g + b


def _sigmoid(v):
    return 1.0 / (1.0 + jnp.exp(-v))


def _tree(op, parts):
    while len(parts) > 1:
        parts = [op(a, b) for a, b in zip(parts[::2], parts[1::2])] + ([parts[-1]] if len(parts) % 2 else [])
    return parts[0]


def _reduce_rows(op, v):
    slab = _tree(op, [v[r:r + SUBLANES] for r in range(0, v.shape[0], SUBLANES)])
    return op.reduce(slab, axis=0, keepdims=True)


def _rope_group(h, c, sa, sb, half):
    return h * c + pltpu.roll(h, LANES - half, 1) * sa + pltpu.roll(h, half, 1) * sb


QA_GROUPS = A_HEADS * A_HEAD_DIM // LANES
QI_GROUPS = IDX_HEADS * IDX_DIM // LANES


def _proj_a_kernel(x_ref, w_ref, c_ref, sa_ref, sb_ref, qa_ref, qi_ref, ka_ref, ki_ref, va_ref, wi_ref):
    xb = x_ref[...].astype(BF16)
    h = jnp.dot(xb, w_ref[...], preferred_element_type=F32)
    c, sa, sb = c_ref[...], sa_ref[...], sb_ref[...]
    half = A_HEAD_DIM // 2
    group = lambda g: h[:, g * LANES:(g + 1) * LANES]
    for g in range(QA_GROUPS):
        qa_ref[:, g * LANES:(g + 1) * LANES] = _rope_group(group(g), c, sa, sb, half).astype(BF16)
    off = QA_GROUPS
    for g in range(QI_GROUPS):
        qi_ref[:, g * LANES:(g + 1) * LANES] = _rope_group(group(off + g), c, sa, sb, half).astype(BF16)
    off += QI_GROUPS
    ka_ref[...] = _rope_group(group(off), c, sa, sb, half).astype(BF16)
    ki_ref[...] = _rope_group(group(off + 1), c, sa, sb, half).astype(BF16)
    va_ref[0] = group(off + 2).T[:A_HEAD_DIM, :].astype(BF16)
    wi_ref[0] = group(off + 3).T[:SUBLANES, :]


def _proj_a(x2, w_a, tabs, tm):
    T, D = x2.shape
    n = w_a.shape[1]
    row = lambda w: pl.BlockSpec((tm, w), lambda i: (i, 0))
    return pl.pallas_call(
        _proj_a_kernel,
        grid=(T // tm,),
        in_specs=[row(D), pl.BlockSpec((D, n), lambda i: (0, 0)), row(LANES), row(LANES), row(LANES)],
        out_specs=[row(QA_GROUPS * LANES), row(QI_GROUPS * LANES), row(LANES), row(LANES),
                   pl.BlockSpec((1, A_HEAD_DIM, tm), lambda i: (i, 0, 0)),
                   pl.BlockSpec((1, SUBLANES, tm), lambda i: (i, 0, 0))],
        out_shape=[jax.ShapeDtypeStruct((T, QA_GROUPS * LANES), BF16),
                   jax.ShapeDtypeStruct((T, QI_GROUPS * LANES), BF16),
                   jax.ShapeDtypeStruct((T, LANES), BF16),
                   jax.ShapeDtypeStruct((T, LANES), BF16),
                   jax.ShapeDtypeStruct((T // tm, A_HEAD_DIM, tm), BF16),
                   jax.ShapeDtypeStruct((T // tm, SUBLANES, tm), F32)],
        compiler_params=_cparams(("parallel",)),
        name="proj_a",
    )(x2, w_a, *tabs)


def _proj_b_kernel(x_ref, w_ref, gq_ref, gkv_ref, wuq_ref, wukv_ref, c_ref, sa_ref, sb_ref,
                   q_ref, k_ref, v_ref, *, q_scale):
    xb = x_ref[...].astype(BF16)
    h = jnp.dot(xb, w_ref[...], preferred_element_type=F32)
    c, sa, sb = c_ref[...], sa_ref[...], sb_ref[...]
    half = B_ROPE // 2
    cq = h[:, :Q_LORA]
    cqn = cq * lax.rsqrt(jnp.mean(cq * cq, axis=-1, keepdims=True) + RMS_EPS) * gq_ref[...]
    ckv = h[:, Q_LORA:Q_LORA + KV_LORA]
    ckvn = ckv * lax.rsqrt(jnp.mean(ckv * ckv, axis=-1, keepdims=True) + RMS_EPS) * gkv_ref[...]
    kr = _rope_group(h[:, Q_LORA + KV_LORA:], c, sa, sb, half)
    q = jnp.dot(cqn.astype(BF16), wuq_ref[...], preferred_element_type=F32)
    kv = jnp.dot(ckvn.astype(BF16), wukv_ref[...], preferred_element_type=F32)
    for g in range(B_HEADS):
        qg = q[:, g * LANES:(g + 1) * LANES]
        q_ref[:, g * LANES:(g + 1) * LANES] = (_rope_group(qg, c, sa, sb, half) * q_scale).astype(BF16)
        k_ref[:, g * LANES:(g + 1) * LANES] = (kv[:, g * LANES:(g + 1) * LANES] + kr).astype(BF16)
    v_ref[0] = kv[:, B_HEADS * LANES:].T.astype(BF16)


def _proj_b(x2, w_b, gq, gkv, wuq, wukv, tabs, tm):
    T, D = x2.shape
    row = lambda w: pl.BlockSpec((tm, w), lambda i: (i, 0))
    full = lambda a: pl.BlockSpec(a.shape, lambda i: (0, 0))
    q_scale = float((B_NOPE + B_ROPE) ** -0.5 * LOG2E)
    return pl.pallas_call(
        functools.partial(_proj_b_kernel, q_scale=q_scale),
        grid=(T // tm,),
        in_specs=[row(D), full(w_b), full(gq), full(gkv), full(wuq), full(wukv),
                  row(LANES), row(LANES), row(LANES)],
        out_specs=[row(B_HEADS * LANES), row(B_HEADS * LANES),
                   pl.BlockSpec((1, B_HEADS * B_V, tm), lambda i: (i, 0, 0))],
        out_shape=[jax.ShapeDtypeStruct((T, B_HEADS * LANES), BF16),
                   jax.ShapeDtypeStruct((T, B_HEADS * LANES), BF16),
                   jax.ShapeDtypeStruct((T // tm, B_HEADS * B_V, tm), BF16)],
        compiler_params=_cparams(("parallel",)),
        name="proj_b",
    )(x2, w_b, gq, gkv, wuq, wukv, *tabs)


CONV_HALO = 32
CONV_ROWS = 64


def _conv_kernel(x_ref, w_ref, cw_ref, cb_ref, g_ref, b_ref, o_ref, hbuf, *, tm):
    j = pl.program_id(1)
    span = CONV_HALO + tm

    @pl.when(j == 0)
    def _():
        hbuf[0, 0:CONV_HALO, :] = jnp.zeros((CONV_HALO, C_CH), F32)

    @pl.when(j > 0)
    def _():
        hbuf[0, 0:CONV_HALO, :] = hbuf[0, tm:span, :]

    xb = x_ref[...].astype(BF16)
    u = jnp.dot(xb, w_ref[...], preferred_element_type=F32)
    hbuf[0, CONV_HALO:span, :] = u[:, :C_CH] * _sigmoid(u[:, C_CH:])
    for r in range(1, SUBLANES):
        hbuf[r, SUBLANES:span, :] = hbuf[0, SUBLANES - r:span - r, :]
    for sb in range(tm // CONV_ROWS):
        r0 = sb * CONV_ROWS
        acc = jnp.zeros((CONV_ROWS, C_CH), F32) + cb_ref[...]
        for t in range(CONV_W):
            back = CONV_W - 1 - t
            r = back % SUBLANES
            s0 = CONV_HALO + r0 - back + r
            acc = acc + hbuf[r, s0:s0 + CONV_ROWS, :] * cw_ref[t:t + 1, :]
        y = _layer_norm(acc, g_ref[...], b_ref[...])
        o_ref[r0:r0 + CONV_ROWS, :] = (y * _sigmoid(y)).astype(BF16)


def _conv_branch(x2, w_c, cw, cb, g, b, batch, seq, tm):
    T, D = x2.shape
    nj = seq // tm
    row = lambda w: pl.BlockSpec((tm, w), lambda bi, j: (bi * nj + j, 0))
    full = lambda a: pl.BlockSpec(a.shape, lambda bi, j: (0, 0))
    return pl.pallas_call(
        functools.partial(_conv_kernel, tm=tm),
        grid=(batch, nj),
        in_specs=[row(D), full(w_c), full(cw), full(cb), full(g), full(b)],
        out_specs=row(C_CH),
        out_shape=jax.ShapeDtypeStruct((T, C_CH), BF16),
        scratch_shapes=[pltpu.VMEM((SUBLANES, tm + CONV_HALO, C_CH), F32)],
        compiler_params=_cparams(("arbitrary", "arbitrary")),
        name="conv_branch",
    )(x2, w_c, cw, cb, g, b)


WORD = 32
ONES_ROWS = 16


def _bit_planes(words):
    a = list(words)
    masks = {16: 0x0000FFFF, 8: 0x00FF00FF, 4: 0x0F0F0F0F, 2: 0x33333333, 1: 0x55555555}

    def swap(k, j):
        t = (a[k] ^ lax.shift_right_logical(a[k + j], jnp.int32(j))) & jnp.int32(masks[j])
        a[k] = a[k] ^ t
        a[k + j] = a[k + j] ^ lax.shift_left(t, jnp.int32(j))

    for base in range(0, WORD, 8):
        for j in (4, 2, 1):
            for k in range(base, base + 8):
                if not k & j:
                    swap(k, j)
    for r in range(8):
        for j in (16, 8):
            for k in range(r, WORD, 8):
                if not k & j:
                    swap(k, j)
    return a


def _dsa_kernel(qa_ref, qi_ref, wi_ref, ka_ref, ki_ref, vt_ref, o_ref, keys_ref, planes_ref, m_ref, acc_ref,
                sa_ref, sb_ref, *, tq, kc, top_k, idx_bits):
    i = pl.program_id(1)
    nkc = ((i + 1) * tq + kc - 1) // kc
    int_min = jnp.int32(INT_MIN)
    kthf = jnp.float32(top_k)
    nt = (((1,), (1,)), ((), ()))
    wrows = kc // WORD
    nwords = planes_ref.shape[1]

    low_half = lax.broadcasted_iota(I32, (1, LANES), 1) < A_HEAD_DIM

    def head_of(q2, h):
        grp = q2[:, (h // 2) * LANES:(h // 2 + 1) * LANES]
        return jnp.where(low_half if h % 2 == 0 else jnp.logical_not(low_half), grp, jnp.zeros_like(grp))

    qi = qi_ref[...]
    qi_st = jnp.concatenate([head_of(qi, h) for h in range(IDX_HEADS)], axis=0)
    w = wi_ref[0]
    q_pos = i * tq + lax.broadcasted_iota(I32, (kc, tq), 1)
    k_iota = lax.broadcasted_iota(I32, (kc, tq), 0)

    @pl.when(i == 0)
    def _():
        planes_ref[...] = jnp.zeros(planes_ref.shape, I32)

    def score_chunk(c, diagonal):
        k0 = pl.multiple_of(c * kc, kc)
        kblk = ki_ref[pl.ds(k0, kc), :]
        lg = lax.dot_general(kblk, qi_st, nt, preferred_element_type=F32)
        sc = w[0:1, :] * jnp.maximum(lg[:, 0:tq], 0.0)
        for h in range(1, IDX_HEADS):
            sc = sc + w[h:h + 1, :] * jnp.maximum(lg[:, h * tq:(h + 1) * tq], 0.0)
        bits = lax.bitcast_convert_type(sc, I32)
        key = jnp.where(bits < 0, int_min - bits, bits)
        if diagonal:
            key = jnp.where(k0 + k_iota <= q_pos, key, int_min)
        keys_ref[c] = key
        w0 = pl.multiple_of(c * wrows, wrows)
        for g in range(wrows // SUBLANES):
            base = g * WORD * SUBLANES
            planes = _bit_planes([key[base + SUBLANES * j:base + SUBLANES * (j + 1)] for j in range(WORD)])
            planes[0] = ~planes[0]
            for s in range(WORD):
                planes_ref[s, pl.ds(w0 + SUBLANES * g, SUBLANES), :] = planes[s]

    def score_body(c, carry):
        score_chunk(c, False)
        return carry

    lax.fori_loop(0, nkc - 1, score_body, 0)
    score_chunk(nkc - 1, True)

    word_row = lax.broadcasted_iota(I32, (nwords, tq), 0)
    act0 = jnp.where(word_row < nkc * wrows, jnp.int32(-1), jnp.int32(0))

    def popcount_rows(v):
        cnt = _tree(jnp.add, [lax.population_count(v[r:r + SUBLANES]) for r in range(0, nwords, SUBLANES)])
        return jnp.sum(cnt.astype(F32), axis=0, keepdims=True)

    def bit_body(s, carry):
        act, rem, thr_u = carry
        cand = act & planes_ref[s]
        c1 = popcount_rows(cand)
        take = c1 >= rem
        act = jnp.where(take, cand, act ^ cand)
        rem = jnp.where(take, rem, rem - c1)
        thr_u = thr_u | jnp.where(take, lax.shift_left(jnp.int32(1), jnp.int32(31) - s), jnp.int32(0))
        return act, rem, thr_u

    act, rem, thr_u = lax.fori_loop(0, WORD, bit_body,
                                    (act0, jnp.full((1, tq), kthf, F32), jnp.zeros((1, tq), I32)))
    thr = thr_u ^ int_min

    need = jnp.logical_and(popcount_rows(act) > rem, thr > int_min)
    any_need = jnp.max(jnp.where(need, 1.0, 0.0)) > 0.0

    def last_tied_index():
        sub_bits = SUBLANES.bit_length() - 1
        group_bits = (WORD * SUBLANES).bit_length() - 1
        row_index = (word_row & (SUBLANES - 1)) | lax.shift_left(lax.shift_right_logical(word_row, sub_bits),
                                                                 group_bits)
        tied, left, j0 = act, rem, jnp.zeros((1, tq), I32)
        for b in reversed(range(idx_bits)):
            if sub_bits <= b < group_bits:
                pattern = sum(1 << j for j in range(WORD) if ((WORD - 1 - j) >> (b - sub_bits)) & 1)
                plane = jnp.int32(pattern - 2 ** 32 if pattern >= 2 ** 31 else pattern)
            else:
                plane = -(lax.shift_right_logical(row_index, b) & 1)
            low = tied & ~plane
            n_low = popcount_rows(low)
            stay = n_low >= left
            tied = jnp.where(stay, low, tied & plane)
            left = jnp.where(stay, left, left - n_low)
            j0 = j0 | jnp.where(stay, jnp.int32(0), jnp.int32(1 << b))
        return j0

    j0 = lax.cond(any_need, last_tied_index, lambda: jnp.zeros((1, tq), I32))
    j0 = jnp.where(need, j0, jnp.where(thr > int_min, jnp.int32(2 ** 31 - 1), jnp.int32(-1)))

    q = qa_ref[...]
    sub = LANES
    eye = jnp.where(lax.broadcasted_iota(I32, (sub, sub), 0) == lax.broadcasted_iota(I32, (sub, sub), 1),
                    1.0, 0.0).astype(BF16)
    npairs = A_HEADS // 2
    units = [(u, hp) for u in range(tq // sub) for hp in range(npairs)]
    q_unit = [jnp.concatenate([jnp.concatenate([head_of(q[u * sub:(u + 1) * sub], h), eye], axis=1)
                               for h in (2 * hp, 2 * hp + 1)], axis=0) for u, hp in units]
    cols = [slice(n * 2 * sub, (n + 1) * 2 * sub) for n in range(len(units))]
    ones = jnp.ones((ONES_ROWS, kc), BF16)
    m_ref[...] = jnp.full(m_ref.shape, NEG_BIG, F32)
    acc_ref[...] = jnp.zeros(acc_ref.shape, F32)

    def keys_aug(c):
        cc = jnp.minimum(c, nkc - 1)
        k0 = pl.multiple_of(cc * kc, kc)
        kb = keys_ref[cc]
        tie_bias = jnp.where(k0 + k_iota <= j0, 0.0, NEG_BIG)
        bias = jnp.where(kb > thr, 0.0, jnp.where(kb == thr, tie_bias, NEG_BIG)).astype(BF16)
        kblk = ka_ref[pl.ds(k0, kc), :]
        return [jnp.concatenate([kblk, bias[:, u * sub:(u + 1) * sub]], axis=1)
                for u in range(tq // sub)]

    def scores(k_aug, n, s_ref):
        s_ref[:, cols[n]] = lax.dot_general(k_aug[units[n][0]], q_unit[n], nt,
                                            preferred_element_type=F32)

    def softmax(n, s_ref):
        s = s_ref[:, cols[n]]
        m_prev = m_ref[:, cols[n]]
        m_new = jnp.maximum(m_prev, jnp.max(s, axis=0, keepdims=True))
        m_ref[:, cols[n]] = m_new
        return jnp.exp2(m_prev - m_new), jnp.exp2(s - m_new).astype(BF16)

    def weighted_values(vt_aug, n, alpha, p):
        acc_ref[:, cols[n]] = alpha * acc_ref[:, cols[n]] + jnp.dot(vt_aug, p, preferred_element_type=F32)

    def half_trip(c, cur_ref, nxt_ref):
        k_aug = keys_aug(c + 1)
        vt_aug = jnp.concatenate([vt_ref[c], ones], axis=0)
        pending = None
        for n in range(len(units)):
            scores(k_aug, n, nxt_ref)
            if pending is not None:
                weighted_values(vt_aug, n - 1, *pending)
            pending = softmax(n, cur_ref)
        weighted_values(vt_aug, len(units) - 1, *pending)

    def att_body(j, carry):
        half_trip(2 * j, sa_ref, sb_ref)
        half_trip(2 * j + 1, sb_ref, sa_ref)
        return carry

    k_aug0 = keys_aug(0)
    for n in range(len(units)):
        scores(k_aug0, n, sa_ref)
    lax.fori_loop(0, nkc // 2, att_body, 0)

    @pl.when(nkc % 2 == 1)
    def _():
        vt_aug = jnp.concatenate([vt_ref[nkc - 1], ones], axis=0)
        for n in range(len(units)):
            weighted_values(vt_aug, n, *softmax(n, sa_ref))

    out = acc_ref[0:A_HEAD_DIM, :] / acc_ref[A_HEAD_DIM:A_HEAD_DIM + 1, :]
    for n, (u, hp) in enumerate(units):
        o2 = out[:, cols[n]]
        pair = jnp.concatenate([o2[:, :sub], o2[:, sub:]], axis=0)
        o_ref[u * sub:(u + 1) * sub, hp * LANES:(hp + 1) * LANES] = pair.T.astype(BF16)


def _dsa_attention(qa, qi, wit, ka2, ki2, vat, batch, seq, tq, kc):
    T = qa.shape[0]
    nq = seq // tq
    r = kc // tq
    top_k = min(TOPK_MAX, seq // 4)
    qrow = lambda w: pl.BlockSpec((tq, w), lambda b, i: (b * nq + i, 0))
    return pl.pallas_call(
        functools.partial(_dsa_kernel, tq=tq, kc=kc, top_k=top_k, idx_bits=(seq - 1).bit_length()),
        grid=(batch, nq),
        in_specs=[qrow(QA_GROUPS * LANES), qrow(QI_GROUPS * LANES),
                  pl.BlockSpec((1, SUBLANES, tq), lambda b, i: ((b * nq + i) // r, 0, (b * nq + i) % r)),
                  pl.BlockSpec((seq, LANES), lambda b, i: (b, 0)),
                  pl.BlockSpec((seq, LANES), lambda b, i: (b, 0)),
                  pl.BlockSpec((seq // kc, A_HEAD_DIM, kc), lambda b, i: (b, 0, 0))],
        out_specs=qrow(A_HEADS * A_HEAD_DIM),
        out_shape=jax.ShapeDtypeStruct((T, A_HEADS * A_HEAD_DIM), BF16),
        scratch_shapes=[pltpu.VMEM((seq // kc, kc, tq), I32),
                        pltpu.VMEM((WORD, seq // WORD, tq), I32),
                        pltpu.VMEM((1, A_HEADS * tq), F32),
                        pltpu.VMEM((A_HEAD_DIM + ONES_ROWS, A_HEADS * tq), F32),
                        pltpu.VMEM((kc, A_HEADS * tq), F32),
                        pltpu.VMEM((kc, A_HEADS * tq), F32)],
        compiler_params=_cparams(("arbitrary", "arbitrary")),
        name="dsa_attention",
    )(qa, qi, wit, ka2, ki2, vat)


MLA_HEADS_PER_STEP = 4


def _mla_kernel(q_ref, k_ref, vt_ref, o_ref, m_ref, acc_ref, sa_ref, sb_ref, *, tq, nh):
    i = pl.program_id(2)
    nt = (((1,), (1,)), ((), ()))
    k_pos = lax.broadcasted_iota(I32, (tq, tq), 0)
    q_pos = lax.broadcasted_iota(I32, (tq, tq), 1)
    ones = jnp.ones((ONES_ROWS, tq), BF16)
    m_ref[...] = jnp.full(m_ref.shape, NEG_BIG, F32)
    acc_ref[...] = jnp.zeros(acc_ref.shape, F32)

    def scores(c, hh, s_ref):
        k0 = pl.multiple_of(c * tq, tq)
        s_ref[hh] = lax.dot_general(k_ref[pl.ds(k0, tq), hh * LANES:(hh + 1) * LANES],
                                    q_ref[:, hh * LANES:(hh + 1) * LANES], nt,
                                    preferred_element_type=F32)

    def softmax_pv(c, hh, s_ref, diagonal):
        vt_aug = jnp.concatenate([vt_ref[c, hh * B_V:(hh + 1) * B_V, :], ones], axis=0)
        s = s_ref[hh]
        if diagonal:
            s = jnp.where(k_pos <= q_pos, s, NEG_BIG)
        m_prev = m_ref[hh]
        m_new = jnp.maximum(m_prev, jnp.max(s, axis=0, keepdims=True))
        alpha = jnp.exp2(m_prev - m_new)
        p = jnp.exp2(s - m_new).astype(BF16)
        acc_ref[hh] = alpha * acc_ref[hh] + jnp.dot(vt_aug, p, preferred_element_type=F32)
        m_ref[hh] = m_new

    def half_trip(cur, cur_ref, nxt, nxt_ref, diagonal=False):
        for hh in range(nh):
            scores(nxt, hh, nxt_ref)
            softmax_pv(cur, hh, cur_ref, diagonal)

    for hh in range(nh):
        scores(i, hh, sa_ref)
    half_trip(i, sa_ref, 0, sb_ref, diagonal=True)

    def trip(t, carry):
        half_trip(2 * t, sb_ref, 2 * t + 1, sa_ref)
        half_trip(2 * t + 1, sa_ref, 2 * t + 2, sb_ref)
        return carry

    lax.fori_loop(0, i // 2, trip, 0)

    @pl.when(i % 2 == 1)
    def _():
        for hh in range(nh):
            softmax_pv(i - 1, hh, sb_ref, False)

    outs = [acc_ref[hh, 0:B_V, :] / acc_ref[hh, B_V:B_V + 1, :] for hh in range(nh)]
    o_ref[...] = jnp.concatenate(outs, axis=0).T.astype(BF16)


def _mla_attention(q, k, vt, batch, seq, tq):
    T = q.shape[0]
    nq = seq // tq
    nh = MLA_HEADS_PER_STEP
    return pl.pallas_call(
        functools.partial(_mla_kernel, tq=tq, nh=nh),
        grid=(batch, B_HEADS // nh, nq),
        in_specs=[pl.BlockSpec((tq, nh * LANES), lambda b, h, i: (b * nq + i, h)),
                  pl.BlockSpec((seq, nh * LANES), lambda b, h, i: (b, h)),
                  pl.BlockSpec((nq, nh * B_V, tq), lambda b, h, i: (b, h, 0))],
        out_specs=pl.BlockSpec((tq, nh * B_V), lambda b, h, i: (b * nq + i, h)),
        out_shape=jax.ShapeDtypeStruct((T, B_HEADS * B_V), BF16),
        scratch_shapes=[pltpu.VMEM((nh, 1, tq), F32), pltpu.VMEM((nh, B_V + ONES_ROWS, tq), F32),
                        pltpu.VMEM((nh, tq, tq), F32), pltpu.VMEM((nh, tq, tq), F32)],
        compiler_params=_cparams(("parallel", "parallel", "arbitrary")),
        name="mla_attention",
    )(q, k, vt)


def _merge_kernel(x_ref, ya_ref, yb_ref, yc_ref, wg_ref, bg_ref, wbr_ref, wo_ref, g_ref, b_ref, o_ref,
                  *, alpha, d):
    x = x_ref[...]
    xb = x.astype(BF16)
    merged = None
    for k, y_ref in enumerate((ya_ref, yb_ref, yc_ref)):
        pre = jnp.dot(xb, wg_ref[:, k * d:(k + 1) * d], preferred_element_type=F32) + bg_ref[k:k + 1, :]
        br = jnp.dot(y_ref[...], wbr_ref[k], preferred_element_type=F32)
        term = _sigmoid(pre) * br
        merged = term if merged is None else merged + term
    mix = jnp.dot(merged.astype(BF16), wo_ref[...], preferred_element_type=F32)
    o_ref[...] = _layer_norm(alpha * x + mix, g_ref[...], b_ref[...])


def _merge(x2, ya, yb, yc, wg, bg, wbr, wo, g, b, alpha, tm):
    T, D = x2.shape
    row = lambda w: pl.BlockSpec((tm, w), lambda i: (i, 0))
    full = lambda a: pl.BlockSpec(a.shape, lambda i: (0,) * a.ndim, pipeline_mode=pl.Buffered(1))
    return pl.pallas_call(
        functools.partial(_merge_kernel, alpha=alpha, d=D),
        grid=(T // tm,),
        in_specs=[row(D), row(ya.shape[1]), row(yb.shape[1]), row(yc.shape[1]),
                  full(wg), full(bg), full(wbr), full(wo), full(g), full(b)],
        out_specs=row(D),
        out_shape=jax.ShapeDtypeStruct((T, D), F32),
        compiler_params=_cparams(("parallel",)),
        name="merge_ln1",
    )(x2, ya, yb, yc, wg, bg, wbr, wo, g, b)


def _ffn_kernel(x_ref, p_ref, wg_ref, wu_ref, wo_ref, wpg_ref, wpp_ref, g_ref, b_ref, o_ref, *, alpha, tf):
    x = x_ref[...]
    xb = x.astype(BF16)
    nchunk = wg_ref.shape[1] // tf

    def hidden(c):
        fg = jnp.dot(xb, wg_ref[:, c * tf:(c + 1) * tf], preferred_element_type=F32)
        fu = jnp.dot(xb, wu_ref[:, c * tf:(c + 1) * tf], preferred_element_type=F32)
        return (fg * _sigmoid(fg) * fu).astype(BF16)

    def down(a, c):
        return jnp.dot(a, wo_ref[c * tf:(c + 1) * tf, :], preferred_element_type=F32)

    gate = _sigmoid(jnp.dot(xb, wpg_ref[...], preferred_element_type=F32))
    emb = jnp.dot(p_ref[...].astype(BF16), wpp_ref[...], preferred_element_type=F32)
    acc = alpha * x + gate * emb
    a_prev = hidden(0)
    for c in range(1, nchunk):
        a_cur = hidden(c)
        acc = acc + down(a_prev, c - 1)
        a_prev = a_cur
    acc = acc + down(a_prev, nchunk - 1)
    o_ref[...] = _layer_norm(acc, g_ref[...], b_ref[...])


def _ffn(x2, p3, layer, wg, wu, wo, wpg, wpp, g, b, alpha, tm):
    T, D = x2.shape
    row = lambda w: pl.BlockSpec((tm, w), lambda i: (i, 0))
    full = lambda a: pl.BlockSpec(a.shape, lambda i: (0,) * a.ndim, pipeline_mode=pl.Buffered(1))
    return pl.pallas_call(
        functools.partial(_ffn_kernel, alpha=alpha, tf=_ffn_tile(wg.shape[1])),
        grid=(T // tm,),
        in_specs=[row(D), pl.BlockSpec((None, tm, p3.shape[2]), lambda i: (layer, i, 0)),
                  full(wg), full(wu), full(wo), full(wpg), full(wpp), full(g), full(b)],
        out_specs=row(D),
        out_shape=jax.ShapeDtypeStruct((T, D), F32),
        compiler_params=_cparams(("parallel",)),
        name="ffn_ple_ln2",
    )(x2, p3, wg, wu, wo, wpg, wpp, g, b)


def _rope_tables(positions, dim, lane_lo, period):
    half = dim // 2
    inv = 1.0 / (ROPE_THETA ** (jnp.arange(0, dim, 2, dtype=F32) / dim))
    ang = positions.reshape(-1).astype(F32)[:, None] * inv
    cos = jnp.tile(jnp.cos(ang), (1, LANES // half))
    sin = jnp.tile(jnp.sin(ang), (1, LANES // half))
    rel = jnp.arange(LANES) % period - lane_lo
    inside = (rel >= 0) & (rel < dim)
    lo = inside & (rel < half)
    hi = inside & (rel >= half)
    c = jnp.where(inside[None], cos, 1.0)
    sa = jnp.where(lo[None], -sin, 0.0)
    sb = jnp.where(hi[None], sin, 0.0)
    return c, sa, sb


def _pad_cols(w, n):
    return jnp.pad(w, ((0, 0), (0, n - w.shape[1])))


def _ffn_tile(ff):
    for tf in (512, 256, 128):
        if ff % tf == 0:
            return tf
    return ff


def _prep_layer(w_in, w_uq, w_ukv, w_ffn_in, layer):
    d = w_in.shape[1]
    sizes = (A_HEADS * A_HEAD_DIM, A_HEAD_DIM, A_HEAD_DIM, IDX_HEADS * IDX_DIM, IDX_DIM, IDX_HEADS,
             Q_LORA, KV_LORA, B_ROPE, 2 * C_CH, N_BRANCH * d)
    bounds = [sum(sizes[:k]) for k in range(len(sizes) + 1)]
    qa, ka, va, qi, ki, wi, cq, ckv, kr, conv, gate = (w_in[layer, :, bounds[k]:bounds[k + 1]]
                                                       for k in range(len(sizes)))
    w_uq, w_ukv = w_uq[layer], w_ukv[layer]
    bf = lambda a: a.astype(BF16)
    qa = bf(qa * (A_HEAD_DIM ** -0.5 * LOG2E))
    wi = bf(wi * (IDX_DIM ** -0.5 * IDX_HEADS ** -0.5))
    w_a = jnp.concatenate([qa, bf(qi), bf(ka), bf(ka), bf(ki), bf(ki), _pad_cols(bf(va), LANES),
                           _pad_cols(wi, LANES)], axis=1)
    kr = jnp.pad(bf(kr), ((0, 0), (B_NOPE, LANES - B_NOPE - B_ROPE)))
    w_b = jnp.concatenate([bf(cq), bf(ckv), kr], axis=1)
    wuq = jnp.pad(bf(w_uq).reshape(-1, B_HEADS, B_NOPE + B_ROPE),
                  ((0, 0), (0, 0), (0, LANES - B_NOPE - B_ROPE))).reshape(-1, B_HEADS * LANES)
    ukv = bf(w_ukv).reshape(-1, B_HEADS, B_NOPE + B_V)
    wuk = jnp.pad(ukv[:, :, :B_NOPE], ((0, 0), (0, 0), (0, LANES - B_NOPE))).reshape(-1, B_HEADS * LANES)
    wuv = ukv[:, :, B_NOPE:].reshape(-1, B_HEADS * B_V)
    wukv = jnp.concatenate([wuk, wuv], axis=1)
    ff = w_ffn_in.shape[2] // 2
    return dict(w_a=w_a, w_b=w_b, w_c=bf(conv), w_g=bf(gate), wuq=wuq, wukv=wukv,
                wfg=bf(w_ffn_in[layer, :, :ff]), wfu=bf(w_ffn_in[layer, :, ff:]))


def kernel(x, p, positions, w_in, b_gate, q_norm_g, w_uq, kv_norm_g, w_ukv, conv_w, conv_b, conv_ln_g,
           conv_ln_b, w_branch, w_out, ln1_g, ln1_b, w_ffn_in, w_ffn_out, w_ple_gate, w_ple_proj, ln2_g, ln2_b):
    batch, seq, d = x.shape
    depth = w_in.shape[0]
    T = batch * seq
    alpha = float((2 * depth) ** 0.25)
    tm = min(512, seq)
    tq_a = min(256, seq)
    assert seq % tm == 0 and tm % tq_a == 0 and tq_a % LANES == 0 and tm % (WORD * SUBLANES) == 0, (seq, tm, tq_a)
    tabs_a = _rope_tables(positions, A_HEAD_DIM, 0, A_HEAD_DIM)
    tabs_b = _rope_tables(positions, B_ROPE, B_NOPE, LANES)
    row = lambda v: v.reshape(1, -1)
    x2 = x.reshape(T, d)
    for i in range(depth):
        w = _prep_layer(w_in, w_uq, w_ukv, w_ffn_in, i)
        qa, qi, ka2, ki2, vat, wit = _proj_a(x2, w["w_a"], tabs_a, tm)
        qb, kb, vbt = _proj_b(x2, w["w_b"], row(q_norm_g[i]), row(kv_norm_g[i]), w["wuq"], w["wukv"], tabs_b, tm)
        y_c = _conv_branch(x2, w["w_c"], conv_w[i], row(conv_b[i]), row(conv_ln_g[i]), row(conv_ln_b[i]),
                           batch, seq, tm)
        y_a = _dsa_attention(qa, qi, wit, ka2, ki2, vat, batch, seq, tq_a, tm)
        y_b = _mla_attention(qb, kb, vbt, batch, seq, tm)
        x2 = _merge(x2, y_a, y_b, y_c, w["w_g"], b_gate[i], w_branch[i].astype(BF16), w_out[i].astype(BF16),
                    row(ln1_g[i]), row(ln1_b[i]), alpha, tm)
        x2 = _ffn(x2, p.reshape(depth, T, -1), i, w["wfg"], w["wfu"], w_ffn_out[i].astype(BF16),
                  w_ple_gate[i].astype(BF16), w_ple_proj[i].astype(BF16), row(ln2_g[i]), row(ln2_b[i]),
                  alpha, tm)
    return x2.reshape(batch, seq, d)
```
